```python
import math
import jax, jax.numpy as jnp
from jax import lax
import numpy as np

D_MODEL = 1024
BATCH = 8
SEQ = 4096
DEPTH = 4

D_FF = 2816
A_GROUPS = 4
A_GROUP_DIM = 128
A_WIDTH = A_GROUPS * A_GROUP_DIM
SGU_CHUNK = 128
B_HEADS = 8
HEAD_DIM = 64
B_WIDTH = B_HEADS * HEAD_DIM
MIX_WIDTH = A_WIDTH + B_WIDTH
IN_WIDTH = 2 * A_WIDTH + 3 * B_WIDTH
DILATED_BRANCHES = ((128, 1), (512, 4), (2048, 16))
HALF_STEPS = 64
ATT_BLOCK = 64
ROPE_THETA = 10000.0
POOL_WINDOWS = (2, 4, 8, 16)
POOL_GROUP_DIM = D_MODEL // len(POOL_WINDOWS)
N_EVEN = (DEPTH + 1) // 2
N_ODD = DEPTH // 2
RMS_EPS = 1e-6
NEG_BIG = -1e30

kernel_name = 'hybrid_sgu_dilated_pool_encoder'


def _rmsnorm(t, gain):
    tf = t.astype(jnp.float32)
    tf = tf * lax.rsqrt(jnp.mean(tf * tf, axis=-1, keepdims=True) + RMS_EPS)
    return (tf * gain.astype(jnp.float32)).astype(t.dtype)


def _swiglu(h, w_gate, w_up, w_down):
    return (jax.nn.silu(h @ w_gate) * (h @ w_up)) @ w_down


def _rope_tables(seq_len):
    pos = jnp.arange(seq_len, dtype=jnp.float32)
    inv_freq = ROPE_THETA ** (-jnp.arange(0, HEAD_DIM, 2, dtype=jnp.float32) / HEAD_DIM)
    ang = pos[:, None] * inv_freq[None, :]
    ang = jnp.concatenate([ang, ang], axis=-1)[:, None, :]
    return jnp.cos(ang), jnp.sin(ang)


def _apply_rope(t, cos, sin):
    half = HEAD_DIM // 2
    rot = jnp.concatenate([-t[..., half:], t[..., :half]], axis=-1)
    return (t.astype(jnp.float32) * cos + rot.astype(jnp.float32) * sin).astype(t.dtype)


def _neighbour_blocks(t, axis):
    pad = [(0, 0)] * t.ndim
    pad[axis] = (1, 1)
    tp = jnp.pad(t, pad)
    n = t.shape[axis]
    parts = [lax.slice_in_dim(tp, i, i + n, axis=axis) for i in range(3)]
    return jnp.concatenate(parts, axis=axis + 1)


def _dilated_branch(q, k, v, dil):
    bsz, nh, s, hd = q.shape
    span = dil * ATT_BLOCK
    sp = -(-s // span) * span
    pad = ((0, 0), (0, 0), (0, sp - s), (0, 0))
    length = sp // dil
    nb = length // ATT_BLOCK

    def to_residue(t):
        t = jnp.pad(t, pad).astype(jnp.float32)
        return t.reshape(bsz, nh, length, dil, hd).transpose(0, 1, 3, 2, 4)

    qr, kr, vr = to_residue(q), to_residue(k), to_residue(v)
    valid = (jnp.arange(sp) < s).reshape(length, dil).T
    qb = qr.reshape(bsz, nh, dil, nb, ATT_BLOCK, hd)
    kb = _neighbour_blocks(kr.reshape(bsz, nh, dil, nb, ATT_BLOCK, hd), 3)
    vb = _neighbour_blocks(vr.reshape(bsz, nh, dil, nb, ATT_BLOCK, hd), 3)
    validb = _neighbour_blocks(valid.reshape(dil, nb, ATT_BLOCK), 1)
    rel = jnp.arange(3 * ATT_BLOCK)[None, :] - ATT_BLOCK - jnp.arange(ATT_BLOCK)[:, None]
    band = jnp.abs(rel) <= HALF_STEPS
    mask = band[None, None] & validb[:, :, None, :]
    scores = jnp.einsum('bhrnqd,bhrnkd->bhrnqk', qb, kb) * (1.0 / math.sqrt(hd))
    scores = jnp.where(mask, scores, NEG_BIG)
    m = jnp.max(scores, axis=-1, keepdims=True)
    p = jnp.exp(scores - m)
    den = jnp.sum(p, axis=-1)
    o = jnp.einsum('bhrnqk,bhrnkd->bhrnqd', p, vb) / den[..., None]
    lse = m[..., 0] + jnp.log(den)
    o = o.reshape(bsz, nh, dil, length, hd).transpose(0, 1, 3, 2, 4).reshape(bsz, nh, sp, hd)[:, :, :s]
    lse = lse.reshape(bsz, nh, dil, length).transpose(0, 1, 3, 2).reshape(bsz, nh, sp)[:, :, :s]
    return o, lse


def _dilated_attention(q, k, v):
    outs, lses = [], []
    for window, dil in DILATED_BRANCHES:
        o, lse = _dilated_branch(q, k, v, dil)
        outs.append(o)
        lses.append(lse)
    wts = jax.nn.softmax(jnp.stack(lses, axis=0), axis=0)
    out = jnp.sum(wts[..., None] * jnp.stack(outs, axis=0), axis=0)
    return out.astype(q.dtype)


def _even_mixer(h, w_in, sgu_norm, w_spatial, b_spatial, q_norm, k_norm, w_out):
    bsz, s, _ = h.shape
    z = h @ w_in
    zu = z[..., :A_WIDTH]
    zv = z[..., A_WIDTH:2 * A_WIDTH]
    zq = z[..., 2 * A_WIDTH:2 * A_WIDTH + B_WIDTH]
    zk = z[..., 2 * A_WIDTH + B_WIDTH:2 * A_WIDTH + 2 * B_WIDTH]
    za = z[..., 2 * A_WIDTH + 2 * B_WIDTH:]
    u = jax.nn.gelu(zu, approximate=False)
    gv = _rmsnorm(jax.nn.gelu(zv, approximate=False).reshape(bsz, s, A_GROUPS, A_GROUP_DIM), sgu_norm)
    gv = gv.reshape(bsz, s // SGU_CHUNK, SGU_CHUNK, A_GROUPS, A_GROUP_DIM)
    mixed = jnp.einsum('gts,bcsgd->bctgd', w_spatial, gv) + b_spatial.T[None, None, :, :, None]
    a_out = u * mixed.reshape(bsz, s, A_WIDTH)
    cos, sin = _rope_tables(s)
    q = _apply_rope(_rmsnorm(zq.reshape(bsz, s, B_HEADS, HEAD_DIM), q_norm), cos, sin)
    k = _apply_rope(_rmsnorm(zk.reshape(bsz, s, B_HEADS, HEAD_DIM), k_norm), cos, sin)
    va = za.reshape(bsz, s, B_HEADS, HEAD_DIM)
    b_out = _dilated_attention(q.transpose(0, 2, 1, 3), k.transpose(0, 2, 1, 3), va.transpose(0, 2, 1, 3))
    b_out = b_out.transpose(0, 2, 1, 3).reshape(bsz, s, B_WIDTH)
    return jnp.concatenate([a_out, b_out], axis=-1) @ w_out


def _pool_mixer(h, w_group, scale):
    bsz, s, d = h.shape
    hf = h.astype(jnp.float32)
    cs = jnp.concatenate([jnp.zeros((bsz, 1, d), jnp.float32), lax.cumsum(hf, axis=1)], axis=1)
    idx = jnp.arange(s)
    outs = []
    for g, win in enumerate(POOL_WINDOWS):
        lo = win // 2
        hi = win - 1 - lo
        start = jnp.clip(idx - lo, 0, s)
        end = jnp.clip(idx + hi + 1, 0, s)
        sl = slice(g * POOL_GROUP_DIM, (g + 1) * POOL_GROUP_DIM)
        csg = cs[..., sl]
        cnt = (end - start).astype(jnp.float32)[None, :, None]
        pooled = (csg[:, end] - csg[:, start]) / cnt
        y = (pooled - hf[..., sl]).astype(h.dtype)
        outs.append(y @ w_group[g])
    return jnp.concatenate(outs, axis=-1) * scale


def setup_inputs(seed: int = 0) -> dict:
    key = jax.random.key(seed)
    ks = jax.random.split(key, 20)

    def nrm(k, shape, sc):
        return jax.random.normal(k, shape, jnp.float32) * sc

    def gain(k, shape):
        return 1.0 + 0.02 * jax.random.normal(k, shape, jnp.float32)

    return {
        'x': nrm(ks[0], (BATCH, SEQ, D_MODEL), 1.0),
        'ffn1_norm': gain(ks[1], (DEPTH, D_MODEL)),
        'ffn1_w_gate': nrm(ks[2], (DEPTH, D_MODEL, D_FF), D_MODEL ** -0.5),
        'ffn1_w_up': nrm(ks[3], (DEPTH, D_MODEL, D_FF), D_MODEL ** -0.5),
        'ffn1_w_down': nrm(ks[4], (DEPTH, D_FF, D_MODEL), D_FF ** -0.5),
        'mix_norm': gain(ks[5], (DEPTH, D_MODEL)),
        'even_w_in': nrm(ks[6], (N_EVEN, D_MODEL, IN_WIDTH), D_MODEL ** -0.5),
        'sgu_norm': gain(ks[7], (N_EVEN, A_GROUPS, A_GROUP_DIM)),
        'sgu_w_spatial': nrm(ks[8], (N_EVEN, A_GROUPS, SGU_CHUNK, SGU_CHUNK), SGU_CHUNK ** -0.5),
        'sgu_b_spatial': 1.0 + 0.1 * jax.random.normal(ks[9], (N_EVEN, A_GROUPS, SGU_CHUNK), jnp.float32),
        'attn_q_norm': gain(ks[10], (N_EVEN, HEAD_DIM)),
        'attn_k_norm': gain(ks[11], (N_EVEN, HEAD_DIM)),
        'even_w_out': nrm(ks[12], (N_EVEN, MIX_WIDTH, D_MODEL), MIX_WIDTH ** -0.5),
        'pool_w_group': nrm(ks[13], (N_ODD, len(POOL_WINDOWS), POOL_GROUP_DIM, POOL_GROUP_DIM), POOL_GROUP_DIM ** -0.5),
        'pool_scale': gain(ks[14], (N_ODD, D_MODEL)),
        'ffn2_norm': gain(ks[15], (DEPTH, D_MODEL)),
        'ffn2_w_gate': nrm(ks[16], (DEPTH, D_MODEL, D_FF), D_MODEL ** -0.5),
        'ffn2_w_up': nrm(ks[17], (DEPTH, D_MODEL, D_FF), D_MODEL ** -0.5),
        'ffn2_w_down': nrm(ks[18], (DEPTH, D_FF, D_MODEL), D_FF ** -0.5),
    }


def reference(x, ffn1_norm, ffn1_w_gate, ffn1_w_up, ffn1_w_down, mix_norm, even_w_in, sgu_norm,
              sgu_w_spatial, sgu_b_spatial, attn_q_norm, attn_k_norm, even_w_out, pool_w_group,
              pool_scale, ffn2_norm, ffn2_w_gate, ffn2_w_up, ffn2_w_down):
    for layer in range(DEPTH):
        x = x + 0.5 * _swiglu(_rmsnorm(x, ffn1_norm[layer]), ffn1_w_gate[layer], ffn1_w_up[layer], ffn1_w_down[layer])
        h = _rmsnorm(x, mix_norm[layer])
        j = layer // 2
        if layer % 2 == 0:
            x = x + _even_mixer(h, even_w_in[j], sgu_norm[j], sgu_w_spatial[j], sgu_b_spatial[j],
                                attn_q_norm[j], attn_k_norm[j], even_w_out[j])
        else:
            x = x + _pool_mixer(h, pool_w_group[j], pool_scale[j])
        x = x + 0.5 * _swiglu(_rmsnorm(x, ffn2_norm[layer]), ffn2_w_gate[layer], ffn2_w_up[layer], ffn2_w_down[layer])
    return x
```

```python
import functools
import math

import jax
import jax.numpy as jnp
import numpy as np
from jax import lax
from jax.experimental import pallas as pl
from jax.experimental.pallas import tpu as pltpu

LANES = 128
SUBLANES = 8
VMEM_LIMIT_BYTES = 56 * 1024 * 1024

D_MODEL = 1024
D_FF = 2816
A_GROUPS = 4
A_GROUP_DIM = 128
A_WIDTH = A_GROUPS * A_GROUP_DIM
SGU_CHUNK = 128
B_HEADS = 8
HEAD_DIM = 64
B_WIDTH = B_HEADS * HEAD_DIM
IN_WIDTH = 2 * A_WIDTH + 3 * B_WIDTH
DILATIONS = (1, 4, 16)
HALF_STEPS = 64
ROPE_THETA = 10000.0
POOL_WINDOWS = (2, 4, 8, 16)
POOL_GROUP_DIM = D_MODEL // len(POOL_WINDOWS)
RMS_EPS = 1e-6
NEG_BIG = -1e30

BF16 = jnp.bfloat16
F32 = jnp.float32

FFN_TM = 512
FFN_FC = 256
PROJ_TM = 256
OUT_TM = 512
POOL_TM = 256
POOL_HALO = 8
ATT_QB = 128
ATT_KW = ATT_QB + 2 * HALF_STEPS
ATT_PAD = HALF_STEPS * max(DILATIONS)


def _params(n_grid_axes):
    return pltpu.CompilerParams(
        dimension_semantics=("arbitrary",) * n_grid_axes,
        vmem_limit_bytes=VMEM_LIMIT_BYTES)


def _resident(shape):
    nd = len(shape)
    return pl.BlockSpec(shape, lambda *_: (0,) * nd, pipeline_mode=pl.Buffered(1))


def _rms_rows(x, gain):
    ms = jnp.mean(x * x, axis=-1, keepdims=True)
    return x * lax.rsqrt(ms + RMS_EPS) * gain


def _ffn_body(x_ref, gain_ref, wg_ref, wu_ref, wd_ref, o_ref, acc_ref):
    x = x_ref[...]
    h = _rms_rows(x, gain_ref[...]).astype(BF16)
    for c in range(D_FF // FFN_FC):
        cols = slice(c * FFN_FC, (c + 1) * FFN_FC)
        g = jnp.dot(h, wg_ref[:, cols], preferred_element_type=F32)
        u = jnp.dot(h, wu_ref[:, cols], preferred_element_type=F32)
        a = (g * jax.nn.sigmoid(g) * u).astype(BF16)
        part = jnp.dot(a, wd_ref[cols, :], preferred_element_type=F32)
        if c == 0:
            acc_ref[...] = part
        else:
            acc_ref[...] += part
    o_ref[...] = x + 0.5 * acc_ref[...]


def _ffn(x, gain, wg, wu, wd):
    t, d = x.shape
    row = pl.BlockSpec((FFN_TM, d), lambda i: (i, 0))
    return pl.pallas_call(
        _ffn_body,
        out_shape=jax.ShapeDtypeStruct((t, d), F32),
        grid=(t // FFN_TM,),
        in_specs=[row, _resident((1, d)), _resident(wg.shape), _resident(wu.shape),
                  _resident(wd.shape)],
        out_specs=row,
        scratch_shapes=[pltpu.VMEM((FFN_TM, d), F32)],
        compiler_params=_params(1),
        name="ffn",
    )(x, gain.reshape(1, d), wg, wu, wd)


def _gelu(z):
    return 0.5 * z * (1.0 + lax.erf(z * (1.0 / math.sqrt(2.0))))


def _head_mean_sq(z, hm_ref):
    sq = z * z
    hi = sq.astype(BF16)
    lo = (sq - hi.astype(F32)).astype(BF16)
    return (jnp.dot(hi, hm_ref[...], preferred_element_type=F32)
            + jnp.dot(lo, hm_ref[...], preferred_element_type=F32))


def _rope(t, cos, sin_up, sin_dn):
    half = HEAD_DIM // 2
    return (t * cos + pltpu.roll(t, half, axis=1) * sin_up
            + pltpu.roll(t, LANES - half, axis=1) * sin_dn)


def _proj_body(x_ref, gain_ref, win_ref, sgun_ref, ws_ref, bs_ref, qn_ref, kn_ref, hm_ref,
               cos_ref, sup_ref, sdn_ref, a_ref, q_ref, k_ref, v_ref):
    h = _rms_rows(x_ref[...], gain_ref[...]).astype(BF16)

    def proj(lo, width):
        return jnp.dot(h, win_ref[:, lo:lo + width], preferred_element_type=F32)

    u = _gelu(proj(0, A_WIDTH))
    gv = _gelu(proj(A_WIDTH, A_WIDTH))
    n_chunks = PROJ_TM // SGU_CHUNK
    for g in range(A_GROUPS):
        cols = slice(g * A_GROUP_DIM, (g + 1) * A_GROUP_DIM)
        gvn = _rms_rows(gv[:, cols], sgun_ref[:, cols]).astype(BF16)
        rhs = jnp.concatenate(
            [gvn[c * SGU_CHUNK:(c + 1) * SGU_CHUNK, :] for c in range(n_chunks)], axis=1)
        mixed = jnp.dot(ws_ref[g], rhs, preferred_element_type=F32)
        for c in range(n_chunks):
            rows = slice(c * SGU_CHUNK, (c + 1) * SGU_CHUNK)
            m = mixed[:, c * A_GROUP_DIM:(c + 1) * A_GROUP_DIM] + bs_ref[g]
            a_ref[rows, cols] = (u[rows, cols] * m).astype(a_ref.dtype)

    zq = proj(2 * A_WIDTH, B_WIDTH)
    zk = proj(2 * A_WIDTH + B_WIDTH, B_WIDTH)
    qn = zq * lax.rsqrt(_head_mean_sq(zq, hm_ref) + RMS_EPS) * qn_ref[...]
    kn = zk * lax.rsqrt(_head_mean_sq(zk, hm_ref) + RMS_EPS) * kn_ref[...]
    cos, sup, sdn = cos_ref[...], sup_ref[...], sdn_ref[...]
    scale = 1.0 / math.sqrt(HEAD_DIM)
    for j in range(B_WIDTH // LANES):
        cols = slice(j * LANES, (j + 1) * LANES)
        q_ref[:, cols] = _rope(qn[:, cols], cos, sup, sdn) * scale
        k_ref[:, cols] = _rope(kn[:, cols], cos, sup, sdn)
    v_ref[...] = proj(2 * A_WIDTH + 2 * B_WIDTH, B_WIDTH)


def _rope_tables(seq_len):
    pos = jnp.arange(seq_len, dtype=F32)
    inv_freq = ROPE_THETA ** (-jnp.arange(0, HEAD_DIM, 2, dtype=F32) / HEAD_DIM)
    ang = pos[:, None] * inv_freq[None, :]
    ang = jnp.concatenate([ang, ang] * (LANES // HEAD_DIM), axis=-1)
    upper = (jnp.arange(LANES) % HEAD_DIM) >= HEAD_DIM // 2
    sin = jnp.sin(ang)
    return jnp.cos(ang), jnp.where(upper, sin, 0.0), jnp.where(upper, 0.0, -sin)


def _even_proj(x, seq_len, gain, w_in, sgu_norm, w_spatial, b_spatial, q_norm, k_norm):
    t, d = x.shape
    tiles_per_seq = seq_len // PROJ_TM
    cos, sup, sdn = _rope_tables(seq_len)
    head_mean = jnp.asarray(
        np.kron(np.eye(B_HEADS), np.full((HEAD_DIM, HEAD_DIM), 1.0 / HEAD_DIM)), BF16)
    b_full = jnp.broadcast_to(b_spatial[:, :, None], (A_GROUPS, SGU_CHUNK, A_GROUP_DIM))
    row = lambda w: pl.BlockSpec((PROJ_TM, w), lambda i: (i, 0))
    tab = pl.BlockSpec((PROJ_TM, LANES), lambda i: (i % tiles_per_seq, 0))
    return pl.pallas_call(
        _proj_body,
        out_shape=(jax.ShapeDtypeStruct((t, A_WIDTH), BF16),
                   jax.ShapeDtypeStruct((t, B_WIDTH), F32),
                   jax.ShapeDtypeStruct((t, B_WIDTH), F32),
                   jax.ShapeDtypeStruct((t, B_WIDTH), F32)),
        grid=(t // PROJ_TM,),
        in_specs=[row(d), _resident((1, d)), _resident(w_in.shape), _resident((1, A_WIDTH)),
                  _resident(w_spatial.shape), _resident(b_full.shape),
                  _resident((1, B_WIDTH)), _resident((1, B_WIDTH)),
                  _resident(head_mean.shape), tab, tab, tab],
        out_specs=(row(A_WIDTH), row(B_WIDTH), row(B_WIDTH), row(B_WIDTH)),
        compiler_params=_params(1),
        name="even_proj",
    )(x, gain.reshape(1, d), w_in, sgu_norm.reshape(1, A_WIDTH), w_spatial, b_full,
      jnp.tile(q_norm, B_HEADS).reshape(1, B_WIDTH), jnp.tile(k_norm, B_HEADS).reshape(1, B_WIDTH),
      head_mean, cos, sup, sdn)


def _attn_body(seq_len, q_ref, k_ref, v_ref, bias_ref, o_ref, kpad, vpad, acc, mrun, lrun):
    zeros = jnp.zeros((ATT_PAD, LANES), F32)
    for pad in (kpad, vpad):
        pad[0:ATT_PAD, :] = zeros
        pad[ATT_PAD + seq_len:ATT_PAD + seq_len + ATT_PAD, :] = zeros
    kpad[ATT_PAD:ATT_PAD + seq_len, :] = k_ref[...]
    vpad[ATT_PAD:ATT_PAD + seq_len, :] = v_ref[...]

    lane = lax.broadcasted_iota(jnp.int32, (1, LANES), 1)
    first_head = lane < HEAD_DIM
    keep_a = first_head.astype(F32)
    keep_b = 1.0 - keep_a

    for branch, dil in enumerate(DILATIONS):
        n_blocks = seq_len // dil // ATT_QB

        def block(t, carry, dil=dil, n_blocks=n_blocks, branch=branch):
            r = t // n_blocks
            n = t % n_blocks
            q_start = n * (ATT_QB * dil) + r
            k_start = q_start + (ATT_PAD - HALF_STEPS * dil)
            if dil == 1:
                q_start = pl.multiple_of(q_start, ATT_QB)
                k_start = pl.multiple_of(k_start, HALF_STEPS)
                q_rows = pl.ds(q_start, ATT_QB)
                k_rows = pl.ds(k_start, ATT_KW)
            else:
                q_rows = pl.ds(q_start, ATT_QB, stride=dil)
                k_rows = pl.ds(k_start, ATT_KW, stride=dil)
            q = q_ref[q_rows, :]
            kw = kpad[k_rows, :].astype(BF16)
            vw = vpad[k_rows, :].astype(BF16)
            q2 = jnp.concatenate([q * keep_a, q * keep_b], axis=0).astype(BF16)
            s = lax.dot_general(q2, kw, (((1,), (1,)), ((), ())), preferred_element_type=F32)
            edge = (n == 0).astype(jnp.int32) + 2 * (n == n_blocks - 1).astype(jnp.int32)
            s = s + bias_ref[edge]
            m = jnp.max(s, axis=-1, keepdims=True)
            p = jnp.exp(s - m)
            l = jnp.sum(p, axis=-1, keepdims=True)
            o2 = jnp.dot(p.astype(BF16), vw, preferred_element_type=F32)
            o = jnp.where(first_head, o2[:ATT_QB], o2[ATT_QB:])
            mf = jnp.where(first_head, m[:ATT_QB], m[ATT_QB:])
            lf = jnp.where(first_head, l[:ATT_QB], l[ATT_QB:])
            if branch == 0:
                acc[q_rows, :] = o
                mrun[q_rows, :] = mf
                lrun[q_rows, :] = lf
            else:
                m_old = mrun[q_rows, :]
                m_new = jnp.maximum(m_old, mf)
                w_old = jnp.exp(m_old - m_new)
                w_blk = jnp.exp(mf - m_new)
                acc[q_rows, :] = acc[q_rows, :] * w_old + o * w_blk
                lrun[q_rows, :] = lrun[q_rows, :] * w_old + lf * w_blk
                mrun[q_rows, :] = m_new
            return carry

        lax.fori_loop(0, seq_len // ATT_QB, block, 0)

    o_ref[...] = (acc[...] / lrun[...]).astype(o_ref.dtype)


def _attn_bias():
    qi = np.arange(2 * ATT_QB)[:, None] % ATT_QB
    kj = np.arange(ATT_KW)[None, :]
    band = np.abs(kj - HALF_STEPS - qi) <= HALF_STEPS
    first = kj >= HALF_STEPS
    last = kj < ATT_KW - HALF_STEPS
    variants = (band, band & first, band & last, band & first & last)
    return jnp.asarray(np.stack([np.where(v, 0.0, NEG_BIG) for v in variants]), F32)


def _attention(q, k, v, batch, seq_len):
    t, w = q.shape
    bias = _attn_bias()
    shape3 = (batch, seq_len, w)
    slab = pl.BlockSpec((None, seq_len, LANES), lambda b, j: (b, 0, j))
    padded = pltpu.VMEM((seq_len + 2 * ATT_PAD, LANES), F32)
    state = pltpu.VMEM((seq_len, LANES), F32)
    out = pl.pallas_call(
        functools.partial(_attn_body, seq_len),
        out_shape=jax.ShapeDtypeStruct(shape3, BF16),
        grid=(batch, w // LANES),
        in_specs=[slab, slab, slab, _resident(bias.shape)],
        out_specs=slab,
        scratch_shapes=[padded, padded, state, state, state],
        compiler_params=_params(2),
        name="dilated_attn",
    )(q.reshape(shape3), k.reshape(shape3), v.reshape(shape3), bias)
    return out.reshape(t, w)


def _out_body(x_ref, a_ref, b_ref, w_ref, o_ref):
    o_ref[...] = (x_ref[...]
                  + jnp.dot(a_ref[...], w_ref[0:A_WIDTH, :], preferred_element_type=F32)
                  + jnp.dot(b_ref[...], w_ref[A_WIDTH:, :], preferred_element_type=F32))


def _out_proj(x, a, b, w_out):
    t, d = x.shape
    row = lambda w: pl.BlockSpec((OUT_TM, w), lambda i: (i, 0))
    return pl.pallas_call(
        _out_body,
        out_shape=jax.ShapeDtypeStruct((t, d), F32),
        grid=(t // OUT_TM,),
        in_specs=[row(d), row(A_WIDTH), row(B_WIDTH), _resident(w_out.shape)],
        out_specs=row(d),
        compiler_params=_params(1),
        name="even_out",
    )(x, a, b, w_out)


def _pool_body(seq_len, x_ref, prev_ref, next_ref, gain_ref, band_ref, w_ref, scale_ref, o_ref):
    tiles_per_seq = seq_len // POOL_TM
    tile = pl.program_id(0) % tiles_per_seq
    gain = gain_ref[...]
    x = x_ref[...]
    h = _rms_rows(x, gain)
    h_prev = _rms_rows(prev_ref[...], gain) * (tile > 0).astype(F32)
    h_next = _rms_rows(next_ref[...], gain) * (tile < tiles_per_seq - 1).astype(F32)
    h_ext = jnp.concatenate([h_prev, h, h_next], axis=0)
    ext_hi = h_ext.astype(BF16)
    ext_lo = (h_ext - ext_hi.astype(F32)).astype(BF16)
    pos = tile * POOL_TM + lax.broadcasted_iota(jnp.int32, (POOL_TM, 1), 0)
    for g, win in enumerate(POOL_WINDOWS):
        lo = win // 2
        hi = win - 1 - lo
        cols = slice(g * POOL_GROUP_DIM, (g + 1) * POOL_GROUP_DIM)
        wsum = (jnp.dot(band_ref[g], ext_hi[:, cols], preferred_element_type=F32)
                + jnp.dot(band_ref[g], ext_lo[:, cols], preferred_element_type=F32))
        cnt = jnp.minimum(pos + (hi + 1), seq_len) - jnp.maximum(pos - lo, 0)
        y = (wsum / cnt.astype(F32) - h[:, cols]).astype(BF16)
        mixed = jnp.dot(y, w_ref[g], preferred_element_type=F32)
        o_ref[:, cols] = x[:, cols] + mixed * scale_ref[:, cols]


def _pool_bands():
    qi = np.arange(POOL_TM)[:, None]
    kj = np.arange(POOL_TM + 2 * POOL_HALO)[None, :] - POOL_HALO
    bands = []
    for win in POOL_WINDOWS:
        lo = win // 2
        hi = win - 1 - lo
        bands.append((kj - qi >= -lo) & (kj - qi <= hi))
    return jnp.asarray(np.stack(bands), BF16)


def _pool(x, seq_len, gain, w_group, scale):
    t, d = x.shape
    bands = _pool_bands()
    per_tile = POOL_TM // POOL_HALO
    n_halo_blocks = t // POOL_HALO
    row = pl.BlockSpec((POOL_TM, d), lambda i: (i, 0))
    prev = pl.BlockSpec((POOL_HALO, d), lambda i: (jnp.maximum(i * per_tile - 1, 0), 0))
    nxt = pl.BlockSpec((POOL_HALO, d),
                       lambda i: (jnp.minimum((i + 1) * per_tile, n_halo_blocks - 1), 0))
    return pl.pallas_call(
        functools.partial(_pool_body, seq_len),
        out_shape=jax.ShapeDtypeStruct((t, d), F32),
        grid=(t // POOL_TM,),
        in_specs=[row, prev, nxt, _resident((1, d)), _resident(bands.shape),
                  _resident(w_group.shape), _resident((1, d))],
        out_specs=row,
        compiler_params=_params(1),
        name="pool_mixer",
    )(x, x, x, gain.reshape(1, d), bands, w_group, scale.reshape(1, d))


def kernel(x, ffn1_norm, ffn1_w_gate, ffn1_w_up, ffn1_w_down, mix_norm, even_w_in, sgu_norm,
           sgu_w_spatial, sgu_b_spatial, attn_q_norm, attn_k_norm, even_w_out, pool_w_group,
           pool_scale, ffn2_norm, ffn2_w_gate, ffn2_w_up, ffn2_w_down):
    batch, seq_len, d = x.shape
    depth = ffn1_norm.shape[0]
    assert seq_len % (ATT_QB * max(DILATIONS)) == 0 and d == D_MODEL
    cast = lambda w: w.astype(BF16)
    f1g, f1u, f1d = cast(ffn1_w_gate), cast(ffn1_w_up), cast(ffn1_w_down)
    f2g, f2u, f2d = cast(ffn2_w_gate), cast(ffn2_w_up), cast(ffn2_w_down)
    w_in, w_sp, w_out, w_pool = (cast(even_w_in), cast(sgu_w_spatial), cast(even_w_out),
                                 cast(pool_w_group))
    xf = x.reshape(batch * seq_len, d)
    for layer in range(depth):
        xf = _ffn(xf, ffn1_norm[layer], f1g[layer], f1u[layer], f1d[layer])
        j = layer // 2
        if layer % 2 == 0:
            a, q, k, v = _even_proj(xf, seq_len, mix_norm[layer], w_in[j], sgu_norm[j], w_sp[j],
                                    sgu_b_spatial[j], attn_q_norm[j], attn_k_norm[j])
            b = _attention(q, k, v, batch, seq_len)
            xf = _out_proj(xf, a, b, w_out[j])
        else:
            xf = _pool(xf, seq_len, mix_norm[layer], w_pool[j], pool_scale[j])
        xf = _ffn(xf, ffn2_norm[layer], f2g[layer], f2u[layer], f2d[layer])
    return xf.reshape(batch, seq_len, d)
```

```python
import functools
import math

import jax
import jax.numpy as jnp
import numpy as np
from jax import lax
from jax.experimental import pallas as pl
from jax.experimental.pallas import tpu as pltpu

LANES = 128
SUBLANES = 8
VMEM_LIMIT_BYTES = 56 * 1024 * 1024

D_MODEL = 1024
D_FF = 2816
A_GROUPS = 4
A_GROUP_DIM = 128
A_WIDTH = A_GROUPS * A_GROUP_DIM
SGU_CHUNK = 128
B_HEADS = 8
HEAD_DIM = 64
B_WIDTH = B_HEADS * HEAD_DIM
IN_WIDTH = 2 * A_WIDTH + 3 * B_WIDTH
DILATIONS = (1, 4, 16)
HALF_STEPS = 64
ROPE_THETA = 10000.0
POOL_WINDOWS = (2, 4, 8, 16)
POOL_GROUP_DIM = D_MODEL // len(POOL_WINDOWS)
RMS_EPS = 1e-6
NEG_BIG = -1e30

BF16 = jnp.bfloat16
F32 = jnp.float32

FFN_TM = 512
FFN_FC = 256
PROJ_TM = 256
POOL_HALO = 8
ATT_QB = 128
ATT_KW = ATT_QB + 2 * HALF_STEPS
ATT_PAD = HALF_STEPS * max(DILATIONS)
ATT_GROUP = 4
ATT_MERGE_ROWS = 256


def _params(n_grid_axes):
    return pltpu.CompilerParams(
        dimension_semantics=("arbitrary",) * n_grid_axes,
        vmem_limit_bytes=VMEM_LIMIT_BYTES)


def _resident(shape):
    nd = len(shape)
    return pl.BlockSpec(shape, lambda *_: (0,) * nd, pipeline_mode=pl.Buffered(1))


def _rms_rows(x, gain):
    ms = jnp.mean(x * x, axis=-1, keepdims=True)
    return x * lax.rsqrt(ms + RMS_EPS) * gain


def _swiglu_residual(x, gain_ref, wg_ref, wu_ref, wd_ref, o_ref, acc_ref):
    h = _rms_rows(x, gain_ref[...]).astype(BF16)
    for c in range(D_FF // FFN_FC):
        cols = slice(c * FFN_FC, (c + 1) * FFN_FC)
        g = jnp.dot(h, wg_ref[:, cols], preferred_element_type=F32)
        u = jnp.dot(h, wu_ref[:, cols], preferred_element_type=F32)
        a = (g * jax.nn.sigmoid(g) * u).astype(BF16)
        part = jnp.dot(a, wd_ref[cols, :], preferred_element_type=F32)
        if c == 0:
            acc_ref[...] = part
        else:
            acc_ref[...] += part
    o_ref[...] = x + 0.5 * acc_ref[...]


def _ffn_body(x_ref, gain_ref, wg_ref, wu_ref, wd_ref, o_ref, acc_ref):
    _swiglu_residual(x_ref[...], gain_ref, wg_ref, wu_ref, wd_ref, o_ref, acc_ref)


def _mix_ffn_body(x_ref, a_ref, b_ref, wo_ref, gain_ref, wg_ref, wu_ref, wd_ref, o_ref, acc_ref):
    x = (x_ref[...]
         + jnp.dot(a_ref[...], wo_ref[0:A_WIDTH, :], preferred_element_type=F32)
         + jnp.dot(b_ref[...], wo_ref[A_WIDTH:, :], preferred_element_type=F32))
    _swiglu_residual(x, gain_ref, wg_ref, wu_ref, wd_ref, o_ref, acc_ref)


def _pool_ffn_body(seq_len, x_ref, prev_ref, next_ref, mgain_ref, wp_ref, scale_ref,
                   gain_ref, wg_ref, wu_ref, wd_ref, o_ref, acc_ref, hext_ref):
    tiles_per_seq = seq_len // FFN_TM
    tile = pl.program_id(0) % tiles_per_seq
    mgain = mgain_ref[...]
    x = x_ref[...]
    hext_ref[0:POOL_HALO, :] = _rms_rows(prev_ref[...], mgain) * (tile > 0).astype(F32)
    hext_ref[POOL_HALO:POOL_HALO + FFN_TM, :] = _rms_rows(x, mgain)
    hext_ref[POOL_HALO + FFN_TM:, :] = (_rms_rows(next_ref[...], mgain)
                                        * (tile < tiles_per_seq - 1).astype(F32))
    pos = tile * FFN_TM + lax.broadcasted_iota(jnp.int32, (FFN_TM, 1), 0)
    for g, win in enumerate(POOL_WINDOWS):
        lo = win // 2
        hi = win - 1 - lo
        cols = slice(g * POOL_GROUP_DIM, (g + 1) * POOL_GROUP_DIM)
        wsum = sum(hext_ref[POOL_HALO + k:POOL_HALO + k + FFN_TM, cols]
                   for k in range(-lo, hi + 1))
        cnt = jnp.minimum(pos + (hi + 1), seq_len) - jnp.maximum(pos - lo, 0)
        h = hext_ref[POOL_HALO:POOL_HALO + FFN_TM, cols]
        y = (wsum / cnt.astype(F32) - h).astype(BF16)
        mixed = jnp.dot(y, wp_ref[g], preferred_element_type=F32)
        o_ref[:, cols] = x[:, cols] + mixed * scale_ref[:, cols]
    _swiglu_residual(o_ref[...], gain_ref, wg_ref, wu_ref, wd_ref, o_ref, acc_ref)


def _ffn_call(body, name, x, gain, wg, wu, wd, pre_args=(), pre_specs=(), extra_scratch=()):
    t, d = x.shape
    row = pl.BlockSpec((FFN_TM, d), lambda i: (i, 0))
    return pl.pallas_call(
        body,
        out_shape=jax.ShapeDtypeStruct((t, d), F32),
        grid=(t // FFN_TM,),
        in_specs=[row, *pre_specs, _resident((1, d)), _resident(wg.shape), _resident(wu.shape),
                  _resident(wd.shape)],
        out_specs=row,
        scratch_shapes=[pltpu.VMEM((FFN_TM, d), F32), *extra_scratch],
        compiler_params=_params(1),
        name=name,
    )(x, *pre_args, gain.reshape(1, d), wg, wu, wd)


def _ffn(x, gain, wg, wu, wd):
    return _ffn_call(_ffn_body, "ffn", x, gain, wg, wu, wd)


def _mix_ffn(x, a, b, w_out, gain, wg, wu, wd):
    row = lambda w: pl.BlockSpec((FFN_TM, w), lambda i: (i, 0))
    return _ffn_call(_mix_ffn_body, "even_out_ffn", x, gain, wg, wu, wd,
                     pre_args=(a, b, w_out),
                     pre_specs=(row(A_WIDTH), row(B_WIDTH), _resident(w_out.shape)))


def _pool_ffn(x, seq_len, mix_gain, w_group, scale, gain, wg, wu, wd):
    t, d = x.shape
    per_tile = FFN_TM // POOL_HALO
    n_halo_blocks = t // POOL_HALO
    prev = pl.BlockSpec((POOL_HALO, d), lambda i: (jnp.maximum(i * per_tile - 1, 0), 0))
    nxt = pl.BlockSpec((POOL_HALO, d),
                       lambda i: (jnp.minimum((i + 1) * per_tile, n_halo_blocks - 1), 0))
    return _ffn_call(functools.partial(_pool_ffn_body, seq_len), "pool_ffn", x, gain, wg, wu, wd,
                     pre_args=(x, x, mix_gain.reshape(1, d), w_group, scale.reshape(1, d)),
                     pre_specs=(prev, nxt, _resident((1, d)), _resident(w_group.shape),
                                _resident((1, d))),
                     extra_scratch=(pltpu.VMEM((FFN_TM + 2 * POOL_HALO, d), F32),))


def _gelu(z):
    return 0.5 * z * (1.0 + lax.erf(z * (1.0 / math.sqrt(2.0))))


def _head_mean_sq(z, hm_ref):
    sq = z * z
    hi = sq.astype(BF16)
    lo = (sq - hi.astype(F32)).astype(BF16)
    return (jnp.dot(hi, hm_ref[...], preferred_element_type=F32)
            + jnp.dot(lo, hm_ref[...], preferred_element_type=F32))


def _rope(t, cos, sin_up, sin_dn):
    half = HEAD_DIM // 2
    return (t * cos + pltpu.roll(t, half, axis=1) * sin_up
            + pltpu.roll(t, LANES - half, axis=1) * sin_dn)


def _proj_body(x_ref, gain_ref, win_ref, sgun_ref, ws_ref, bs_ref, qn_ref, kn_ref, hm_ref,
               cos_ref, sup_ref, sdn_ref, a_ref, q_ref, k_ref, v_ref):
    h = _rms_rows(x_ref[...], gain_ref[...]).astype(BF16)

    def proj(lo, width):
        return jnp.dot(h, win_ref[:, lo:lo + width], preferred_element_type=F32)

    u = _gelu(proj(0, A_WIDTH))
    gv = _gelu(proj(A_WIDTH, A_WIDTH))
    n_chunks = PROJ_TM // SGU_CHUNK
    for g in range(A_GROUPS):
        cols = slice(g * A_GROUP_DIM, (g + 1) * A_GROUP_DIM)
        gvn = _rms_rows(gv[:, cols], sgun_ref[:, cols]).astype(BF16)
        rhs = jnp.concatenate(
            [gvn[c * SGU_CHUNK:(c + 1) * SGU_CHUNK, :] for c in range(n_chunks)], axis=1)
        mixed = jnp.dot(ws_ref[g], rhs, preferred_element_type=F32)
        for c in range(n_chunks):
            rows = slice(c * SGU_CHUNK, (c + 1) * SGU_CHUNK)
            m = mixed[:, c * A_GROUP_DIM:(c + 1) * A_GROUP_DIM] + bs_ref[g]
            a_ref[rows, cols] = (u[rows, cols] * m).astype(a_ref.dtype)

    zq = proj(2 * A_WIDTH, B_WIDTH)
    zk = proj(2 * A_WIDTH + B_WIDTH, B_WIDTH)
    qn = zq * lax.rsqrt(_head_mean_sq(zq, hm_ref) + RMS_EPS) * qn_ref[...]
    kn = zk * lax.rsqrt(_head_mean_sq(zk, hm_ref) + RMS_EPS) * kn_ref[...]
    cos, sup, sdn = cos_ref[...], sup_ref[...], sdn_ref[...]
    scale = 1.0 / math.sqrt(HEAD_DIM)
    for j in range(B_WIDTH // LANES):
        cols = slice(j * LANES, (j + 1) * LANES)
        q_ref[:, cols] = _rope(qn[:, cols], cos, sup, sdn) * scale
        k_ref[:, cols] = _rope(kn[:, cols], cos, sup, sdn)
    v_ref[...] = proj(2 * A_WIDTH + 2 * B_WIDTH, B_WIDTH)


def _rope_tables(seq_len):
    pos = jnp.arange(seq_len, dtype=F32)
    inv_freq = ROPE_THETA ** (-jnp.arange(0, HEAD_DIM, 2, dtype=F32) / HEAD_DIM)
    ang = pos[:, None] * inv_freq[None, :]
    ang = jnp.concatenate([ang, ang] * (LANES // HEAD_DIM), axis=-1)
    upper = (jnp.arange(LANES) % HEAD_DIM) >= HEAD_DIM // 2
    sin = jnp.sin(ang)
    return jnp.cos(ang), jnp.where(upper, sin, 0.0), jnp.where(upper, 0.0, -sin)


def _even_proj(x, seq_len, gain, w_in, sgu_norm, w_spatial, b_spatial, q_norm, k_norm):
    t, d = x.shape
    tiles_per_seq = seq_len // PROJ_TM
    cos, sup, sdn = _rope_tables(seq_len)
    head_mean = jnp.asarray(
        np.kron(np.eye(B_HEADS), np.full((HEAD_DIM, HEAD_DIM), 1.0 / HEAD_DIM)), BF16)
    b_full = jnp.broadcast_to(b_spatial[:, :, None], (A_GROUPS, SGU_CHUNK, A_GROUP_DIM))
    row = lambda w: pl.BlockSpec((PROJ_TM, w), lambda i: (i, 0))
    tab = pl.BlockSpec((PROJ_TM, LANES), lambda i: (i % tiles_per_seq, 0))
    return pl.pallas_call(
        _proj_body,
        out_shape=(jax.ShapeDtypeStruct((t, A_WIDTH), BF16),
                   jax.ShapeDtypeStruct((t, B_WIDTH), F32),
                   jax.ShapeDtypeStruct((t, B_WIDTH), F32),
                   jax.ShapeDtypeStruct((t, B_WIDTH), F32)),
        grid=(t // PROJ_TM,),
        in_specs=[row(d), _resident((1, d)), _resident(w_in.shape), _resident((1, A_WIDTH)),
                  _resident(w_spatial.shape), _resident(b_full.shape),
                  _resident((1, B_WIDTH)), _resident((1, B_WIDTH)),
                  _resident(head_mean.shape), tab, tab, tab],
        out_specs=(row(A_WIDTH), row(B_WIDTH), row(B_WIDTH), row(B_WIDTH)),
        compiler_params=_params(1),
        name="even_proj",
    )(x, gain.reshape(1, d), w_in, sgu_norm.reshape(1, A_WIDTH), w_spatial, b_full,
      jnp.tile(q_norm, B_HEADS).reshape(1, B_WIDTH), jnp.tile(k_norm, B_HEADS).reshape(1, B_WIDTH),
      head_mean, cos, sup, sdn)


def _attn_body(seq_len, q_ref, k_ref, v_ref, bias_ref, o_ref, kpad, vpad, num_ref, max_ref, den_ref):
    zeros = jnp.zeros((ATT_PAD, LANES), F32)
    for pad in (kpad, vpad):
        pad[0:ATT_PAD, :] = zeros
        pad[ATT_PAD + seq_len:ATT_PAD + seq_len + ATT_PAD, :] = zeros
    kpad[ATT_PAD:ATT_PAD + seq_len, :] = k_ref[...]
    vpad[ATT_PAD:ATT_PAD + seq_len, :] = v_ref[...]

    lane = lax.broadcasted_iota(jnp.int32, (1, LANES), 1)
    first_head = lane < HEAD_DIM
    keep_a = first_head.astype(F32)
    keep_b = 1.0 - keep_a

    def one_block(branch, dil, n_blocks, flat):
        r = flat // n_blocks
        n = flat % n_blocks
        q_start = n * (ATT_QB * dil) + r
        k_start = q_start + (ATT_PAD - HALF_STEPS * dil)
        if dil == 1:
            q_rows = pl.ds(pl.multiple_of(q_start, ATT_QB), ATT_QB)
            k_rows = pl.ds(pl.multiple_of(k_start, HALF_STEPS), ATT_KW)
        else:
            q_rows = pl.ds(q_start, ATT_QB, stride=dil)
            k_rows = pl.ds(k_start, ATT_KW, stride=dil)
        q = q_ref[q_rows, :]
        kw = kpad[k_rows, :].astype(BF16)
        vw = vpad[k_rows, :].astype(BF16)
        q2 = jnp.concatenate([q * keep_a, q * keep_b], axis=0).astype(BF16)
        s = lax.dot_general(q2, kw, (((1,), (1,)), ((), ())), preferred_element_type=F32)
        edge = (n == 0).astype(jnp.int32) + 2 * (n == n_blocks - 1).astype(jnp.int32)
        s = s + bias_ref[edge]
        m = jnp.max(s, axis=-1, keepdims=True)
        p = jnp.exp(s - m)
        l = jnp.sum(p, axis=-1, keepdims=True)
        o2 = jnp.dot(p.astype(BF16), vw, preferred_element_type=F32)
        num_ref[branch, q_rows, :] = jnp.where(first_head, o2[:ATT_QB], o2[ATT_QB:])
        max_ref[branch, q_rows, :] = jnp.where(first_head, m[:ATT_QB], m[ATT_QB:])
        den_ref[branch, q_rows, :] = jnp.where(first_head, l[:ATT_QB], l[ATT_QB:])

    for branch, dil in enumerate(DILATIONS):
        n_blocks = seq_len // dil // ATT_QB

        def group(t, carry, branch=branch, dil=dil, n_blocks=n_blocks):
            for g in range(ATT_GROUP):
                one_block(branch, dil, n_blocks, t * ATT_GROUP + g)
            return carry

        lax.fori_loop(0, seq_len // ATT_QB // ATT_GROUP, group, 0)

    def merge(c, carry):
        rows = pl.ds(pl.multiple_of(c * ATT_MERGE_ROWS, ATT_MERGE_ROWS), ATT_MERGE_ROWS)
        maxima = [max_ref[b, rows, :] for b in range(len(DILATIONS))]
        m_all = functools.reduce(jnp.maximum, maxima)
        weights = [jnp.exp(m_b - m_all) for m_b in maxima]
        num = sum(w * num_ref[b, rows, :] for b, w in enumerate(weights))
        den = sum(w * den_ref[b, rows, :] for b, w in enumerate(weights))
        o_ref[rows, :] = (num / den).astype(o_ref.dtype)
        return carry

    lax.fori_loop(0, seq_len // ATT_MERGE_ROWS, merge, 0)


def _attn_bias():
    qi = np.arange(2 * ATT_QB)[:, None] % ATT_QB
    kj = np.arange(ATT_KW)[None, :]
    band = np.abs(kj - HALF_STEPS - qi) <= HALF_STEPS
    first = kj >= HALF_STEPS
    last = kj < ATT_KW - HALF_STEPS
    variants = (band, band & first, band & last, band & first & last)
    return jnp.asarray(np.stack([np.where(v, 0.0, NEG_BIG) for v in variants]), F32)


def _attention(q, k, v, batch, seq_len):
    t, w = q.shape
    bias = _attn_bias()
    shape3 = (batch, seq_len, w)
    slab = pl.BlockSpec((None, seq_len, LANES), lambda b, j: (b, 0, j))
    padded = pltpu.VMEM((seq_len + 2 * ATT_PAD, LANES), F32)
    state = pltpu.VMEM((len(DILATIONS), seq_len, LANES), F32)
    out = pl.pallas_call(
        functools.partial(_attn_body, seq_len),
        out_shape=jax.ShapeDtypeStruct(shape3, BF16),
        grid=(batch, w // LANES),
        in_specs=[slab, slab, slab, _resident(bias.shape)],
        out_specs=slab,
        scratch_shapes=[padded, padded, state, state, state],
        compiler_params=_params(2),
        name="dilated_attn",
    )(q.reshape(shape3), k.reshape(shape3), v.reshape(shape3), bias)
    return out.reshape(t, w)


def kernel(x, ffn1_norm, ffn1_w_gate, ffn1_w_up, ffn1_w_down, mix_norm, even_w_in, sgu_norm,
           sgu_w_spatial, sgu_b_spatial, attn_q_norm, attn_k_norm, even_w_out, pool_w_group,
           pool_scale, ffn2_norm, ffn2_w_gate, ffn2_w_up, ffn2_w_down):
    batch, seq_len, d = x.shape
    depth = ffn1_norm.shape[0]
    assert seq_len % (ATT_QB * max(DILATIONS)) == 0 and seq_len % (ATT_QB * ATT_GROUP) == 0
    assert seq_len % FFN_TM == 0 and seq_len % PROJ_TM == 0 and d == D_MODEL
    cast = lambda w: w.astype(BF16)
    f1g, f1u, f1d = cast(ffn1_w_gate), cast(ffn1_w_up), cast(ffn1_w_down)
    f2g, f2u, f2d = cast(ffn2_w_gate), cast(ffn2_w_up), cast(ffn2_w_down)
    w_in, w_sp, w_out, w_pool = (cast(even_w_in), cast(sgu_w_spatial), cast(even_w_out),
                                 cast(pool_w_group))
    xf = x.reshape(batch * seq_len, d)
    for layer in range(depth):
        xf = _ffn(xf, ffn1_norm[layer], f1g[layer], f1u[layer], f1d[layer])
        j = layer // 2
        if layer % 2 == 0:
            a, q, k, v = _even_proj(xf, seq_len, mix_norm[layer], w_in[j], sgu_norm[j], w_sp[j],
                                    sgu_b_spatial[j], attn_q_norm[j], attn_k_norm[j])
            b = _attention(q, k, v, batch, seq_len)
            xf = _mix_ffn(xf, a, b, w_out[j], ffn2_norm[layer], f2g[layer], f2u[layer], f2d[layer])
        else:
            xf = _pool_ffn(xf, seq_len, mix_norm[layer], w_pool[j], pool_scale[j],
                           ffn2_norm[layer], f2g[layer], f2u[layer], f2d[layer])
    return xf.reshape(batch, seq_len, d)
```

```python
import functools
import math
from typing import NamedTuple

import jax
import jax.numpy as jnp
import numpy as np
from jax import lax
from jax.experimental import pallas as pl
from jax.experimental.pallas import tpu as pltpu

LANES = 128
SUBLANES = 8
VMEM_LIMIT_BYTES = 56 * 1024 * 1024

D_MODEL = 1024
D_FF = 2816
A_GROUPS = 4
A_GROUP_DIM = 128
A_WIDTH = A_GROUPS * A_GROUP_DIM
SGU_CHUNK = 128
B_HEADS = 8
HEAD_DIM = 64
B_WIDTH = B_HEADS * HEAD_DIM
IN_WIDTH = 2 * A_WIDTH + 3 * B_WIDTH
DILATIONS = (1, 4, 16)
HALF_STEPS = 64
ROPE_THETA = 10000.0
POOL_WINDOWS = (2, 4, 8, 16)
POOL_GROUP_DIM = D_MODEL // len(POOL_WINDOWS)
RMS_EPS = 1e-6
NEG_BIG = -1e30

BF16 = jnp.bfloat16
F32 = jnp.float32

FFN_TM = 512
FFN_FC = 256
PROJ_TM = 256
POOL_HALO = 8
ATT_QB = 128
ATT_KW = ATT_QB + 2 * HALF_STEPS
ATT_GROUP = 16
ATT_MERGE_ROWS = 256


def _params(n_grid_axes):
    return pltpu.CompilerParams(
        dimension_semantics=("arbitrary",) * n_grid_axes,
        vmem_limit_bytes=VMEM_LIMIT_BYTES)


def _resident(shape):
    nd = len(shape)
    return pl.BlockSpec(shape, lambda *_: (0,) * nd, pipeline_mode=pl.Buffered(1))


class _Layer(NamedTuple):
    stacked: jax.Array
    index: int


def _operand(w):
    if not isinstance(w, _Layer):
        return w, _resident(w.shape)
    shape = w.stacked.shape[1:]
    spec = pl.BlockSpec((None, *shape), lambda *_: (w.index,) + (0,) * len(shape),
                        pipeline_mode=pl.Buffered(1))
    return w.stacked, spec


def _row_vectors(stacked):
    return stacked.reshape(stacked.shape[0], 1, stacked.shape[1])


def _rms_rows(x, gain):
    ms = jnp.mean(x * x, axis=-1, keepdims=True)
    return x * lax.rsqrt(ms + RMS_EPS) * gain


def _swiglu_residual(x, gain_ref, wg_ref, wu_ref, wd_ref, o_ref, acc_ref):
    h = _rms_rows(x, gain_ref[...]).astype(BF16)
    for c in range(D_FF // FFN_FC):
        cols = slice(c * FFN_FC, (c + 1) * FFN_FC)
        g = jnp.dot(h, wg_ref[:, cols], preferred_element_type=F32)
        u = jnp.dot(h, wu_ref[:, cols], preferred_element_type=F32)
        a = (g * jax.nn.sigmoid(g) * u).astype(BF16)
        part = jnp.dot(a, wd_ref[cols, :], preferred_element_type=F32)
        if c == 0:
            acc_ref[...] = part
        else:
            acc_ref[...] += part
    o_ref[...] = x + 0.5 * acc_ref[...]


def _ffn_body(x_ref, gain_ref, wg_ref, wu_ref, wd_ref, o_ref, acc_ref):
    _swiglu_residual(x_ref[...], gain_ref, wg_ref, wu_ref, wd_ref, o_ref, acc_ref)


def _mix_ffn_body(x_ref, a_ref, b_ref, wo_ref, gain_ref, wg_ref, wu_ref, wd_ref, o_ref, acc_ref):
    x = (x_ref[...]
         + jnp.dot(a_ref[...], wo_ref[0:A_WIDTH, :], preferred_element_type=F32)
         + jnp.dot(b_ref[...], wo_ref[A_WIDTH:, :], preferred_element_type=F32))
    _swiglu_residual(x, gain_ref, wg_ref, wu_ref, wd_ref, o_ref, acc_ref)


def _pool_ffn_body(seq_len, x_ref, prev_ref, next_ref, mgain_ref, wp_ref, scale_ref,
                   gain_ref, wg_ref, wu_ref, wd_ref, o_ref, acc_ref, hext_ref):
    tiles_per_seq = seq_len // FFN_TM
    tile = pl.program_id(0) % tiles_per_seq
    mgain = mgain_ref[...]
    x = x_ref[...]
    hext_ref[0:POOL_HALO, :] = _rms_rows(prev_ref[...], mgain) * jnp.where(tile > 0, 1.0, 0.0)
    hext_ref[POOL_HALO:POOL_HALO + FFN_TM, :] = _rms_rows(x, mgain)
    hext_ref[POOL_HALO + FFN_TM:, :] = (_rms_rows(next_ref[...], mgain)
                                        * jnp.where(tile < tiles_per_seq - 1, 1.0, 0.0))
    pos = tile * FFN_TM + lax.broadcasted_iota(jnp.int32, (FFN_TM, 1), 0)
    for g, win in enumerate(POOL_WINDOWS):
        lo = win // 2
        hi = win - 1 - lo
        cols = slice(g * POOL_GROUP_DIM, (g + 1) * POOL_GROUP_DIM)
        wsum = sum(hext_ref[POOL_HALO + k:POOL_HALO + k + FFN_TM, cols]
                   for k in range(-lo, hi + 1))
        cnt = jnp.minimum(pos + (hi + 1), seq_len) - jnp.maximum(pos - lo, 0)
        h = hext_ref[POOL_HALO:POOL_HALO + FFN_TM, cols]
        y = (wsum / cnt.astype(F32) - h).astype(BF16)
        mixed = jnp.dot(y, wp_ref[g], preferred_element_type=F32)
        o_ref[:, cols] = x[:, cols] + mixed * scale_ref[:, cols]
    _swiglu_residual(o_ref[...], gain_ref, wg_ref, wu_ref, wd_ref, o_ref, acc_ref)


def _ffn_call(body, name, x, ffn_params, pre_args=(), pre_specs=(), extra_scratch=()):
    t, d = x.shape
    row = pl.BlockSpec((FFN_TM, d), lambda i: (i, 0))
    arrays, specs = zip(*[_operand(p) for p in ffn_params])
    return pl.pallas_call(
        body,
        out_shape=jax.ShapeDtypeStruct((t, d), F32),
        grid=(t // FFN_TM,),
        in_specs=[row, *pre_specs, *specs],
        out_specs=row,
        scratch_shapes=[pltpu.VMEM((FFN_TM, d), F32), *extra_scratch],
        compiler_params=_params(1),
        name=name,
    )(x, *pre_args, *arrays)


def _ffn(x, ffn_params):
    return _ffn_call(_ffn_body, "ffn", x, ffn_params)


def _mix_ffn(x, a, b, w_out, ffn_params):
    row = lambda w: pl.BlockSpec((FFN_TM, w), lambda i: (i, 0))
    w_arr, w_spec = _operand(w_out)
    return _ffn_call(_mix_ffn_body, "even_out_ffn", x, ffn_params,
                     pre_args=(a, b, w_arr), pre_specs=(row(A_WIDTH), row(B_WIDTH), w_spec))


def _pool_ffn(x, seq_len, pool_params, ffn_params):
    t, d = x.shape
    per_tile = FFN_TM // POOL_HALO
    n_halo_blocks = t // POOL_HALO
    prev = pl.BlockSpec((POOL_HALO, d), lambda i: (jnp.maximum(i * per_tile - 1, 0), 0))
    nxt = pl.BlockSpec((POOL_HALO, d),
                       lambda i: (jnp.minimum((i + 1) * per_tile, n_halo_blocks - 1), 0))
    arrays, specs = zip(*[_operand(p) for p in pool_params])
    return _ffn_call(functools.partial(_pool_ffn_body, seq_len), "pool_ffn", x, ffn_params,
                     pre_args=(x, x, *arrays), pre_specs=(prev, nxt, *specs),
                     extra_scratch=(pltpu.VMEM((FFN_TM + 2 * POOL_HALO, d), F32),))


def _gelu(z):
    return 0.5 * z * (1.0 + lax.erf(z * (1.0 / math.sqrt(2.0))))


def _head_mean_sq(z, hm_ref):
    sq = z * z
    hi = sq.astype(BF16)
    lo = (sq - hi.astype(F32)).astype(BF16)
    return (jnp.dot(hi, hm_ref[...], preferred_element_type=F32)
            + jnp.dot(lo, hm_ref[...], preferred_element_type=F32))


def _rope(t, cos, sin_up, sin_dn):
    half = HEAD_DIM // 2
    return (t * cos + pltpu.roll(t, half, axis=1) * sin_up
            + pltpu.roll(t, LANES - half, axis=1) * sin_dn)


def _proj_body(x_ref, gain_ref, win_ref, sgun_ref, ws_ref, bs_ref, qn_ref, kn_ref, hm_ref,
               cos_ref, sup_ref, sdn_ref, a_ref, q_ref, k_ref, v_ref):
    h = _rms_rows(x_ref[...], gain_ref[...]).astype(BF16)

    def proj(lo, width):
        return jnp.dot(h, win_ref[:, lo:lo + width], preferred_element_type=F32)

    u = _gelu(proj(0, A_WIDTH))
    gv = _gelu(proj(A_WIDTH, A_WIDTH))
    n_chunks = PROJ_TM // SGU_CHUNK
    for g in range(A_GROUPS):
        cols = slice(g * A_GROUP_DIM, (g + 1) * A_GROUP_DIM)
        gvn = _rms_rows(gv[:, cols], sgun_ref[:, cols]).astype(BF16)
        rhs = jnp.concatenate(
            [gvn[c * SGU_CHUNK:(c + 1) * SGU_CHUNK, :] for c in range(n_chunks)], axis=1)
        mixed = jnp.dot(ws_ref[g], rhs, preferred_element_type=F32)
        for c in range(n_chunks):
            rows = slice(c * SGU_CHUNK, (c + 1) * SGU_CHUNK)
            m = mixed[:, c * A_GROUP_DIM:(c + 1) * A_GROUP_DIM] + bs_ref[g]
            a_ref[rows, cols] = (u[rows, cols] * m).astype(a_ref.dtype)

    zq = proj(2 * A_WIDTH, B_WIDTH)
    zk = proj(2 * A_WIDTH + B_WIDTH, B_WIDTH)
    qn = zq * lax.rsqrt(_head_mean_sq(zq, hm_ref) + RMS_EPS) * qn_ref[...]
    kn = zk * lax.rsqrt(_head_mean_sq(zk, hm_ref) + RMS_EPS) * kn_ref[...]
    cos, sup, sdn = cos_ref[...], sup_ref[...], sdn_ref[...]
    scale = 1.0 / math.sqrt(HEAD_DIM)
    for j in range(B_WIDTH // LANES):
        cols = slice(j * LANES, (j + 1) * LANES)
        q_ref[:, cols] = _rope(qn[:, cols], cos, sup, sdn) * scale
        k_ref[:, cols] = _rope(kn[:, cols], cos, sup, sdn)
    v_ref[...] = proj(2 * A_WIDTH + 2 * B_WIDTH, B_WIDTH)


def _rope_tables(seq_len):
    pos = jnp.arange(seq_len, dtype=F32)
    inv_freq = ROPE_THETA ** (-jnp.arange(0, HEAD_DIM, 2, dtype=F32) / HEAD_DIM)
    ang = pos[:, None] * inv_freq[None, :]
    ang = jnp.concatenate([ang, ang] * (LANES // HEAD_DIM), axis=-1)
    upper = (jnp.arange(LANES) % HEAD_DIM) >= HEAD_DIM // 2
    sin = jnp.sin(ang)
    return jnp.cos(ang), jnp.where(upper, sin, 0.0), jnp.where(upper, 0.0, -sin)


def _even_proj(x, seq_len, proj_params):
    t, d = x.shape
    tiles_per_seq = seq_len // PROJ_TM
    cos, sup, sdn = _rope_tables(seq_len)
    head_mean = jnp.asarray(
        np.kron(np.eye(B_HEADS), np.full((HEAD_DIM, HEAD_DIM), 1.0 / HEAD_DIM)), BF16)
    row = lambda w: pl.BlockSpec((PROJ_TM, w), lambda i: (i, 0))
    tab = pl.BlockSpec((PROJ_TM, LANES), lambda i: (i % tiles_per_seq, 0))
    arrays, specs = zip(*[_operand(p) for p in proj_params])
    return pl.pallas_call(
        _proj_body,
        out_shape=(jax.ShapeDtypeStruct((t, A_WIDTH), BF16),
                   jax.ShapeDtypeStruct((t, B_WIDTH), F32),
                   jax.ShapeDtypeStruct((t, B_WIDTH), F32),
                   jax.ShapeDtypeStruct((t, B_WIDTH), F32)),
        grid=(t // PROJ_TM,),
        in_specs=[row(d), *specs, _resident(head_mean.shape), tab, tab, tab],
        out_specs=(row(A_WIDTH), row(B_WIDTH), row(B_WIDTH), row(B_WIDTH)),
        compiler_params=_params(1),
        name="even_proj",
    )(x, *arrays, head_mean, cos, sup, sdn)


def _attn_body(seq_len, q_ref, k_ref, v_ref, bias_ref, o_ref, *scratch):
    n_branches = len(DILATIONS)
    kv_res = [scratch[2 * i:2 * i + 2] for i in range(n_branches)]
    num_ref, max_ref, den_ref = scratch[2 * n_branches:]

    zeros = jnp.zeros((HALF_STEPS, LANES), BF16)
    for (kres, vres), dil in zip(kv_res, DILATIONS):
        seg = seq_len // dil
        pitch = seg + 2 * HALF_STEPS
        for r in range(dil):
            base = r * pitch
            rows = pl.ds(r, seg, stride=dil) if dil > 1 else pl.ds(0, seg)
            for src, dst in ((k_ref, kres), (v_ref, vres)):
                dst[base:base + HALF_STEPS, :] = zeros
                dst[base + HALF_STEPS:base + HALF_STEPS + seg, :] = src[rows, :].astype(BF16)
                dst[base + HALF_STEPS + seg:base + pitch, :] = zeros

    lane = lax.broadcasted_iota(jnp.int32, (1, LANES), 1)
    first_head = lane < HEAD_DIM
    keep_a = first_head.astype(F32)
    keep_b = 1.0 - keep_a

    def one_block(branch, dil, n_blocks, flat):
        r = flat // n_blocks
        n = flat % n_blocks
        q_start = n * (ATT_QB * dil) + r
        if dil == 1:
            q_rows = pl.ds(pl.multiple_of(q_start, ATT_QB), ATT_QB)
        else:
            q_rows = pl.ds(q_start, ATT_QB, stride=dil)
        pitch = seq_len // dil + 2 * HALF_STEPS
        k_rows = pl.ds(pl.multiple_of(r * pitch + n * ATT_QB, ATT_QB), ATT_KW)
        kres, vres = kv_res[branch]
        q = q_ref[q_rows, :]
        kw = kres[k_rows, :]
        vw = vres[k_rows, :]
        q2 = jnp.concatenate([q * keep_a, q * keep_b], axis=0).astype(BF16)
        s = lax.dot_general(q2, kw, (((1,), (1,)), ((), ())), preferred_element_type=F32)
        edge = jnp.where(n == 0, 1, 0) + jnp.where(n == n_blocks - 1, 2, 0)
        s = s + bias_ref[edge]
        m = jnp.max(s, axis=-1, keepdims=True)
        p = jnp.exp(s - m)
        l = jnp.sum(p, axis=-1, keepdims=True)
        o2 = jnp.dot(p.astype(BF16), vw, preferred_element_type=F32)
        num_ref[branch, q_rows, :] = jnp.where(first_head, o2[:ATT_QB], o2[ATT_QB:])
        max_ref[branch, q_rows, :] = jnp.where(first_head, m[:ATT_QB], m[ATT_QB:])
        den_ref[branch, q_rows, :] = jnp.where(first_head, l[:ATT_QB], l[ATT_QB:])

    for branch, dil in enumerate(DILATIONS):
        n_blocks = seq_len // dil // ATT_QB

        def group(t, carry, branch=branch, dil=dil, n_blocks=n_blocks):
            for g in range(ATT_GROUP):
                one_block(branch, dil, n_blocks, t * ATT_GROUP + g)
            return carry

        lax.fori_loop(0, seq_len // ATT_QB // ATT_GROUP, group, 0)

    def merge(c, carry):
        rows = pl.ds(pl.multiple_of(c * ATT_MERGE_ROWS, ATT_MERGE_ROWS), ATT_MERGE_ROWS)
        maxima = [max_ref[b, rows, :] for b in range(len(DILATIONS))]
        m_all = functools.reduce(jnp.maximum, maxima)
        weights = [jnp.exp(m_b - m_all) for m_b in maxima]
        num = sum(w * num_ref[b, rows, :] for b, w in enumerate(weights))
        den = sum(w * den_ref[b, rows, :] for b, w in enumerate(weights))
        o_ref[rows, :] = (num / den).astype(o_ref.dtype)
        return carry

    lax.fori_loop(0, seq_len // ATT_MERGE_ROWS, merge, 0)


def _attn_bias():
    qi = np.arange(2 * ATT_QB)[:, None] % ATT_QB
    kj = np.arange(ATT_KW)[None, :]
    band = np.abs(kj - HALF_STEPS - qi) <= HALF_STEPS
    first = kj >= HALF_STEPS
    last = kj < ATT_KW - HALF_STEPS
    variants = (band, band & first, band & last, band & first & last)
    return jnp.asarray(np.stack([np.where(v, 0.0, NEG_BIG) for v in variants]), F32)


def _attention(q, k, v, batch, seq_len):
    t, w = q.shape
    bias = _attn_bias()
    shape3 = (batch, seq_len, w)
    slab = pl.BlockSpec((None, seq_len, LANES), lambda b, j: (b, 0, j))
    kv_scratch = [pltpu.VMEM((seq_len + 2 * HALF_STEPS * dil, LANES), BF16)
                  for dil in DILATIONS for _ in ("k", "v")]
    state = pltpu.VMEM((len(DILATIONS), seq_len, LANES), F32)
    out = pl.pallas_call(
        functools.partial(_attn_body, seq_len),
        out_shape=jax.ShapeDtypeStruct(shape3, BF16),
        grid=(batch, w // LANES),
        in_specs=[slab, slab, slab, _resident(bias.shape)],
        out_specs=slab,
        scratch_shapes=[*kv_scratch, state, state, state],
        compiler_params=_params(2),
        name="dilated_attn",
    )(q.reshape(shape3), k.reshape(shape3), v.reshape(shape3), bias)
    return out.reshape(t, w)


def kernel(x, ffn1_norm, ffn1_w_gate, ffn1_w_up, ffn1_w_down, mix_norm, even_w_in, sgu_norm,
           sgu_w_spatial, sgu_b_spatial, attn_q_norm, attn_k_norm, even_w_out, pool_w_group,
           pool_scale, ffn2_norm, ffn2_w_gate, ffn2_w_up, ffn2_w_down):
    batch, seq_len, d = x.shape
    depth = ffn1_norm.shape[0]
    assert seq_len % (ATT_QB * max(DILATIONS)) == 0 and seq_len % (ATT_QB * ATT_GROUP) == 0
    assert seq_len % FFN_TM == 0 and seq_len % PROJ_TM == 0 and d == D_MODEL
    cast = lambda w: w.astype(BF16)
    ffn1 = (_row_vectors(ffn1_norm), cast(ffn1_w_gate), cast(ffn1_w_up), cast(ffn1_w_down))
    ffn2 = (_row_vectors(ffn2_norm), cast(ffn2_w_gate), cast(ffn2_w_up), cast(ffn2_w_down))
    mix_gain = _row_vectors(mix_norm)
    n_even = even_w_in.shape[0]
    proj = (cast(even_w_in), sgu_norm.reshape(n_even, 1, A_WIDTH), cast(sgu_w_spatial),
            jnp.broadcast_to(sgu_b_spatial[..., None], sgu_b_spatial.shape + (A_GROUP_DIM,)),
            _row_vectors(jnp.tile(attn_q_norm, (1, B_HEADS))),
            _row_vectors(jnp.tile(attn_k_norm, (1, B_HEADS))))
    w_out = cast(even_w_out)
    pool = (cast(pool_w_group), _row_vectors(pool_scale))

    pick = lambda stacked, index: [_Layer(p, index) for p in stacked]
    xf = x.reshape(batch * seq_len, d)
    for layer in range(depth):
        xf = _ffn(xf, pick(ffn1, layer))
        j = layer // 2
        if layer % 2 == 0:
            a, q, k, v = _even_proj(xf, seq_len, [_Layer(mix_gain, layer), *pick(proj, j)])
            b = _attention(q, k, v, batch, seq_len)
            xf = _mix_ffn(xf, a, b, _Layer(w_out, j), pick(ffn2, layer))
        else:
            xf = _pool_ffn(xf, seq_len, [_Layer(mix_gain, layer), *pick(pool, j)],
                           pick(ffn2, layer))
    return xf.reshape(batch, seq_len, d)
```

```python
import functools
import math
from typing import NamedTuple

import jax
import jax.numpy as jnp
import numpy as np
from jax import lax
from jax.experimental import pallas as pl
from jax.experimental.pallas import tpu as pltpu

LANES = 128
SUBLANES = 8
VMEM_LIMIT_BYTES = 56 * 1024 * 1024

D_MODEL = 1024
D_FF = 2816
A_GROUPS = 4
A_GROUP_DIM = 128
A_WIDTH = A_GROUPS * A_GROUP_DIM
SGU_CHUNK = 128
B_HEADS = 8
HEAD_DIM = 64
B_WIDTH = B_HEADS * HEAD_DIM
IN_WIDTH = 2 * A_WIDTH + 3 * B_WIDTH
DILATIONS = (1, 4, 16)
HALF_STEPS = 64
ROPE_THETA = 10000.0
POOL_WINDOWS = (2, 4, 8, 16)
POOL_GROUP_DIM = D_MODEL // len(POOL_WINDOWS)
RMS_EPS = 1e-6
NEG_BIG = -1e30

BF16 = jnp.bfloat16
F32 = jnp.float32

FFN_TM = 1024
FFN_SUB = 512
FFN_FC = 256
PROJ_TM = 256
POOL_HALO = 8
ATT_QB = 128
ATT_KW = ATT_QB + 2 * HALF_STEPS
ATT_GROUP = 16


def _params(n_grid_axes):
    return pltpu.CompilerParams(
        dimension_semantics=("arbitrary",) * n_grid_axes,
        vmem_limit_bytes=VMEM_LIMIT_BYTES)


def _resident(shape):
    nd = len(shape)
    return pl.BlockSpec(shape, lambda *_: (0,) * nd, pipeline_mode=pl.Buffered(1))


class _Layer(NamedTuple):
    stacked: jax.Array
    index: int


def _operand(w):
    if not isinstance(w, _Layer):
        return w, _resident(w.shape)
    shape = w.stacked.shape[1:]
    spec = pl.BlockSpec((None, *shape), lambda *_: (w.index,) + (0,) * len(shape),
                        pipeline_mode=pl.Buffered(1))
    return w.stacked, spec


def _row_vectors(stacked):
    return stacked.reshape(stacked.shape[0], 1, stacked.shape[1])


def _rms_rows(x, gain):
    ms = jnp.mean(x * x, axis=-1, keepdims=True)
    return x * lax.rsqrt(ms + RMS_EPS) * gain


def _swiglu_residual(x_of_rows, gain_ref, wg_ref, wu_ref, wd_ref, o_ref):
    for s in range(FFN_TM // FFN_SUB):
        rows = slice(s * FFN_SUB, (s + 1) * FFN_SUB)
        x = x_of_rows(rows)
        o_ref[rows, :] = x
        h = _rms_rows(x, gain_ref[...]).astype(BF16)
        for c in range(D_FF // FFN_FC):
            cols = slice(c * FFN_FC, (c + 1) * FFN_FC)
            g = jnp.dot(h, wg_ref[:, cols], preferred_element_type=F32)
            u = jnp.dot(h, wu_ref[:, cols], preferred_element_type=F32)
            a = (g * jax.nn.sigmoid(g) * u).astype(BF16)
            o_ref[rows, :] += jnp.dot(a, wd_ref[cols, :], preferred_element_type=F32)


def _ffn_body(x_ref, *ffn_refs):
    _swiglu_residual(lambda rows: x_ref[rows, :], *ffn_refs)


def _mix_ffn_body(x_ref, a_ref, b_ref, wo_ref, *ffn_refs):
    def mixed_rows(rows):
        return (x_ref[rows, :]
                + jnp.dot(a_ref[rows, :], wo_ref[0:A_WIDTH, :], preferred_element_type=F32)
                + jnp.dot(b_ref[rows, :], wo_ref[A_WIDTH:, :], preferred_element_type=F32))

    _swiglu_residual(mixed_rows, *ffn_refs)


def _pool_ffn_body(seq_len, x_ref, prev_ref, next_ref, mgain_ref, wp_ref, scale_ref,
                   gain_ref, wg_ref, wu_ref, wd_ref, o_ref, hext_ref):
    tiles_per_seq = seq_len // FFN_TM
    tile = pl.program_id(0) % tiles_per_seq
    mgain = mgain_ref[...]
    hext_ref[0:POOL_HALO, :] = _rms_rows(prev_ref[...], mgain) * jnp.where(tile > 0, 1.0, 0.0)
    hext_ref[POOL_HALO:POOL_HALO + FFN_TM, :] = _rms_rows(x_ref[...], mgain)
    hext_ref[POOL_HALO + FFN_TM:, :] = (_rms_rows(next_ref[...], mgain)
                                        * jnp.where(tile < tiles_per_seq - 1, 1.0, 0.0))

    def pooled_rows(rows):
        n_rows = rows.stop - rows.start
        first = POOL_HALO + rows.start
        pos = (tile * FFN_TM + rows.start
               + lax.broadcasted_iota(jnp.int32, (n_rows, 1), 0))
        groups = []
        for g, win in enumerate(POOL_WINDOWS):
            lo = win // 2
            hi = win - 1 - lo
            cols = slice(g * POOL_GROUP_DIM, (g + 1) * POOL_GROUP_DIM)
            wsum = sum(hext_ref[first + k:first + k + n_rows, cols] for k in range(-lo, hi + 1))
            cnt = jnp.minimum(pos + (hi + 1), seq_len) - jnp.maximum(pos - lo, 0)
            h = hext_ref[first:first + n_rows, cols]
            y = (wsum / cnt.astype(F32) - h).astype(BF16)
            mixed = jnp.dot(y, wp_ref[g], preferred_element_type=F32)
            groups.append(x_ref[rows, cols] + mixed * scale_ref[:, cols])
        return jnp.concatenate(groups, axis=1)

    _swiglu_residual(pooled_rows, gain_ref, wg_ref, wu_ref, wd_ref, o_ref)


def _ffn_call(body, name, x, ffn_params, pre_args=(), pre_specs=(), extra_scratch=()):
    t, d = x.shape
    row = pl.BlockSpec((FFN_TM, d), lambda i: (i, 0))
    arrays, specs = zip(*[_operand(p) for p in ffn_params])
    return pl.pallas_call(
        body,
        out_shape=jax.ShapeDtypeStruct((t, d), F32),
        grid=(t // FFN_TM,),
        in_specs=[row, *pre_specs, *specs],
        out_specs=row,
        scratch_shapes=list(extra_scratch),
        compiler_params=_params(1),
        name=name,
    )(x, *pre_args, *arrays)


def _ffn(x, ffn_params):
    return _ffn_call(_ffn_body, "ffn", x, ffn_params)


def _mix_ffn(x, a, b, w_out, ffn_params):
    row = lambda w: pl.BlockSpec((FFN_TM, w), lambda i: (i, 0))
    w_arr, w_spec = _operand(w_out)
    return _ffn_call(_mix_ffn_body, "even_out_ffn", x, ffn_params,
                     pre_args=(a, b, w_arr), pre_specs=(row(A_WIDTH), row(B_WIDTH), w_spec))


def _pool_ffn(x, seq_len, pool_params, ffn_params):
    t, d = x.shape
    per_tile = FFN_TM // POOL_HALO
    n_halo_blocks = t // POOL_HALO
    prev = pl.BlockSpec((POOL_HALO, d), lambda i: (jnp.maximum(i * per_tile - 1, 0), 0))
    nxt = pl.BlockSpec((POOL_HALO, d),
                       lambda i: (jnp.minimum((i + 1) * per_tile, n_halo_blocks - 1), 0))
    arrays, specs = zip(*[_operand(p) for p in pool_params])
    return _ffn_call(functools.partial(_pool_ffn_body, seq_len), "pool_ffn", x, ffn_params,
                     pre_args=(x, x, *arrays), pre_specs=(prev, nxt, *specs),
                     extra_scratch=(pltpu.VMEM((FFN_TM + 2 * POOL_HALO, d), F32),))


def _gelu(z):
    return 0.5 * z * (1.0 + lax.erf(z * (1.0 / math.sqrt(2.0))))


def _head_rms(t, gain, first_head):
    sq = t * t
    ms_a = jnp.sum(jnp.where(first_head, sq, 0.0), axis=-1, keepdims=True) * (1.0 / HEAD_DIM)
    ms_b = jnp.sum(jnp.where(first_head, 0.0, sq), axis=-1, keepdims=True) * (1.0 / HEAD_DIM)
    r = jnp.where(first_head, lax.rsqrt(ms_a + RMS_EPS), lax.rsqrt(ms_b + RMS_EPS))
    return t * r * gain


def _rope(t, cos, sin_up, sin_dn):
    half = HEAD_DIM // 2
    return (t * cos + pltpu.roll(t, half, axis=1) * sin_up
            + pltpu.roll(t, LANES - half, axis=1) * sin_dn)


def _proj_body(x_ref, gain_ref, win_ref, sgun_ref, ws_ref, bs_ref, qn_ref, kn_ref,
               cos_ref, sup_ref, sdn_ref, a_ref, q_ref, k_ref, v_ref):
    h = _rms_rows(x_ref[...], gain_ref[...]).astype(BF16)

    def proj(lo, width):
        return jnp.dot(h, win_ref[:, lo:lo + width], preferred_element_type=F32)

    zq = proj(2 * A_WIDTH, B_WIDTH)
    zk = proj(2 * A_WIDTH + B_WIDTH, B_WIDTH)

    gv = _gelu(proj(A_WIDTH, A_WIDTH))
    u = _gelu(proj(0, A_WIDTH))
    n_chunks = PROJ_TM // SGU_CHUNK
    for g in range(A_GROUPS):
        cols = slice(g * A_GROUP_DIM, (g + 1) * A_GROUP_DIM)
        gvn = _rms_rows(gv[:, cols], sgun_ref[:, cols]).astype(BF16)
        rhs = jnp.concatenate(
            [gvn[c * SGU_CHUNK:(c + 1) * SGU_CHUNK, :] for c in range(n_chunks)], axis=1)
        mixed = jnp.dot(ws_ref[g], rhs, preferred_element_type=F32)
        for c in range(n_chunks):
            rows = slice(c * SGU_CHUNK, (c + 1) * SGU_CHUNK)
            m = mixed[:, c * A_GROUP_DIM:(c + 1) * A_GROUP_DIM] + bs_ref[g]
            a_ref[rows, cols] = (u[rows, cols] * m).astype(a_ref.dtype)

    cos, sup, sdn = cos_ref[...], sup_ref[...], sdn_ref[...]
    first_head = lax.broadcasted_iota(jnp.int32, (1, LANES), 1) < HEAD_DIM
    scale = 1.0 / math.sqrt(HEAD_DIM)
    for j in range(B_WIDTH // LANES):
        cols = slice(j * LANES, (j + 1) * LANES)
        qn = _head_rms(zq[:, cols], qn_ref[:, cols], first_head)
        kn = _head_rms(zk[:, cols], kn_ref[:, cols], first_head)
        q_ref[:, cols] = _rope(qn, cos, sup, sdn) * scale
        k_ref[:, cols] = _rope(kn, cos, sup, sdn)
    v_ref[...] = proj(2 * A_WIDTH + 2 * B_WIDTH, B_WIDTH)


def _rope_tables(seq_len):
    pos = jnp.arange(seq_len, dtype=F32)
    inv_freq = ROPE_THETA ** (-jnp.arange(0, HEAD_DIM, 2, dtype=F32) / HEAD_DIM)
    ang = pos[:, None] * inv_freq[None, :]
    ang = jnp.concatenate([ang, ang] * (LANES // HEAD_DIM), axis=-1)
    upper = (jnp.arange(LANES) % HEAD_DIM) >= HEAD_DIM // 2
    sin = jnp.sin(ang)
    return jnp.cos(ang), jnp.where(upper, sin, 0.0), jnp.where(upper, 0.0, -sin)


def _even_proj(x, seq_len, proj_params):
    t, d = x.shape
    tiles_per_seq = seq_len // PROJ_TM
    cos, sup, sdn = _rope_tables(seq_len)
    row = lambda w: pl.BlockSpec((PROJ_TM, w), lambda i: (i, 0))
    tab = pl.BlockSpec((PROJ_TM, LANES), lambda i: (i % tiles_per_seq, 0))
    arrays, specs = zip(*[_operand(p) for p in proj_params])
    return pl.pallas_call(
        _proj_body,
        out_shape=(jax.ShapeDtypeStruct((t, A_WIDTH), BF16),
                   jax.ShapeDtypeStruct((t, B_WIDTH), F32),
                   jax.ShapeDtypeStruct((t, B_WIDTH), F32),
                   jax.ShapeDtypeStruct((t, B_WIDTH), F32)),
        grid=(t // PROJ_TM,),
        in_specs=[row(d), *specs, tab, tab, tab],
        out_specs=(row(A_WIDTH), row(B_WIDTH), row(B_WIDTH), row(B_WIDTH)),
        compiler_params=_params(1),
        name="even_proj",
    )(x, *arrays, cos, sup, sdn)


def _attn_body(seq_len, q_ref, k_ref, v_ref, bias_ref, o_ref, *scratch):
    n_branches = len(DILATIONS)
    kv_res = [scratch[2 * i:2 * i + 2] for i in range(n_branches)]
    num_ref, max_ref, den_ref = scratch[2 * n_branches:]

    zeros = jnp.zeros((HALF_STEPS, LANES), BF16)
    for (kres, vres), dil in zip(kv_res, DILATIONS):
        seg = seq_len // dil
        pitch = seg + 2 * HALF_STEPS
        for r in range(dil):
            base = r * pitch
            rows = pl.ds(r, seg, stride=dil) if dil > 1 else pl.ds(0, seg)
            for src, dst in ((k_ref, kres), (v_ref, vres)):
                dst[base:base + HALF_STEPS, :] = zeros
                dst[base + HALF_STEPS:base + HALF_STEPS + seg, :] = src[rows, :].astype(BF16)
                dst[base + HALF_STEPS + seg:base + pitch, :] = zeros

    lane = lax.broadcasted_iota(jnp.int32, (1, LANES), 1)
    first_head = lane < HEAD_DIM
    keep_a = first_head.astype(F32)
    keep_b = 1.0 - keep_a

    def one_block(branch, dil, n_blocks, flat):
        r = flat // n_blocks
        n = flat % n_blocks
        q_start = n * (ATT_QB * dil) + r
        if dil == 1:
            q_rows = pl.ds(pl.multiple_of(q_start, ATT_QB), ATT_QB)
        else:
            q_rows = pl.ds(q_start, ATT_QB, stride=dil)
        pitch = seq_len // dil + 2 * HALF_STEPS
        k_rows = pl.ds(pl.multiple_of(r * pitch + n * ATT_QB, ATT_QB), ATT_KW)
        kres, vres = kv_res[branch]
        q = q_ref[q_rows, :]
        kw = kres[k_rows, :]
        vw = vres[k_rows, :]
        q2 = jnp.concatenate([q * keep_a, q * keep_b], axis=0).astype(BF16)
        s = lax.dot_general(q2, kw, (((1,), (1,)), ((), ())), preferred_element_type=F32)
        edge = jnp.where(n == 0, 1, 0) + jnp.where(n == n_blocks - 1, 2, 0)
        s = s + bias_ref[edge]
        m = jnp.max(s, axis=-1, keepdims=True)
        p = jnp.exp(s - m)
        l = jnp.sum(p, axis=-1, keepdims=True)
        o2 = jnp.dot(p.astype(BF16), vw, preferred_element_type=F32)
        nums = [jnp.where(first_head, o2[:ATT_QB], o2[ATT_QB:])]
        maxima = [jnp.where(first_head, m[:ATT_QB], m[ATT_QB:])]
        dens = [jnp.where(first_head, l[:ATT_QB], l[ATT_QB:])]
        if branch > 0:
            num_ref[branch - 1, q_rows, :] = nums[0]
            max_ref[branch - 1, q_rows, :] = maxima[0]
            den_ref[branch - 1, q_rows, :] = dens[0]
            return
        for b in range(n_branches - 1):
            nums.append(num_ref[b, q_rows, :])
            maxima.append(max_ref[b, q_rows, :])
            dens.append(den_ref[b, q_rows, :])
        m_all = functools.reduce(jnp.maximum, maxima)
        weights = [jnp.exp(m_b - m_all) for m_b in maxima]
        num = sum(w * n_b for w, n_b in zip(weights, nums))
        den = sum(w * d_b for w, d_b in zip(weights, dens))
        o_ref[q_rows, :] = (num / den).astype(o_ref.dtype)

    assert DILATIONS[0] == 1
    for branch in reversed(range(n_branches)):
        dil = DILATIONS[branch]
        n_blocks = seq_len // dil // ATT_QB

        def group(t, carry, branch=branch, dil=dil, n_blocks=n_blocks):
            for g in range(ATT_GROUP):
                one_block(branch, dil, n_blocks, t * ATT_GROUP + g)
            return carry

        lax.fori_loop(0, seq_len // ATT_QB // ATT_GROUP, group, 0)


def _attn_bias():
    qi = np.arange(2 * ATT_QB)[:, None] % ATT_QB
    kj = np.arange(ATT_KW)[None, :]
    band = np.abs(kj - HALF_STEPS - qi) <= HALF_STEPS
    first = kj >= HALF_STEPS
    last = kj < ATT_KW - HALF_STEPS
    variants = (band, band & first, band & last, band & first & last)
    return jnp.asarray(np.stack([np.where(v, 0.0, NEG_BIG) for v in variants]), F32)


def _attention(q, k, v, batch, seq_len):
    t, w = q.shape
    bias = _attn_bias()
    shape3 = (batch, seq_len, w)
    slab = pl.BlockSpec((None, seq_len, LANES), lambda b, j: (b, 0, j))
    kv_scratch = [pltpu.VMEM((seq_len + 2 * HALF_STEPS * dil, LANES), BF16)
                  for dil in DILATIONS for _ in ("k", "v")]
    state = pltpu.VMEM((len(DILATIONS) - 1, seq_len, LANES), F32)
    out = pl.pallas_call(
        functools.partial(_attn_body, seq_len),
        out_shape=jax.ShapeDtypeStruct(shape3, BF16),
        grid=(batch, w // LANES),
        in_specs=[slab, slab, slab, _resident(bias.shape)],
        out_specs=slab,
        scratch_shapes=[*kv_scratch, state, state, state],
        compiler_params=_params(2),
        name="dilated_attn",
    )(q.reshape(shape3), k.reshape(shape3), v.reshape(shape3), bias)
    return out.reshape(t, w)


def kernel(x, ffn1_norm, ffn1_w_gate, ffn1_w_up, ffn1_w_down, mix_norm, even_w_in, sgu_norm,
           sgu_w_spatial, sgu_b_spatial, attn_q_norm, attn_k_norm, even_w_out, pool_w_group,
           pool_scale, ffn2_norm, ffn2_w_gate, ffn2_w_up, ffn2_w_down):
    batch, seq_len, d = x.shape
    depth = ffn1_norm.shape[0]
    assert seq_len % (ATT_QB * max(DILATIONS)) == 0 and seq_len % (ATT_QB * ATT_GROUP) == 0
    assert seq_len % FFN_TM == 0 and seq_len % PROJ_TM == 0 and d == D_MODEL
    cast = lambda w: w.astype(BF16)
    ffn1 = (_row_vectors(ffn1_norm), cast(ffn1_w_gate), cast(ffn1_w_up), cast(0.5 * ffn1_w_down))
    ffn2 = (_row_vectors(ffn2_norm), cast(ffn2_w_gate), cast(ffn2_w_up), cast(0.5 * ffn2_w_down))
    mix_gain = _row_vectors(mix_norm)
    n_even = even_w_in.shape[0]
    proj = (cast(even_w_in), sgu_norm.reshape(n_even, 1, A_WIDTH), cast(sgu_w_spatial),
            jnp.broadcast_to(sgu_b_spatial[..., None], sgu_b_spatial.shape + (A_GROUP_DIM,)),
            _row_vectors(jnp.tile(attn_q_norm, (1, B_HEADS))),
            _row_vectors(jnp.tile(attn_k_norm, (1, B_HEADS))))
    w_out = cast(even_w_out)
    pool = (cast(pool_w_group), _row_vectors(pool_scale))

    pick = lambda stacked, index: [_Layer(p, index) for p in stacked]
    xf = x.reshape(batch * seq_len, d)
    for layer in range(depth):
        xf = _ffn(xf, pick(ffn1, layer))
        j = layer // 2
        if layer % 2 == 0:
            a, q, k, v = _even_proj(xf, seq_len, [_Layer(mix_gain, layer), *pick(proj, j)])
            b = _attention(q, k, v, batch, seq_len)
            xf = _mix_ffn(xf, a, b, _Layer(w_out, j), pick(ffn2, layer))
        else:
            xf = _pool_ffn(xf, seq_len, [_Layer(mix_gain, layer), *pick(pool, j)],
                           pick(ffn2, layer))
    return xf.reshape(batch, seq_len, d)
```

```python
import functools
import math
from typing import NamedTuple

import jax
import jax.numpy as jnp
import numpy as np
from jax import lax
from jax.experimental import pallas as pl
from jax.experimental.pallas import tpu as pltpu

LANES = 128
SUBLANES = 8
VMEM_LIMIT_BYTES = 56 * 1024 * 1024

D_MODEL = 1024
D_FF = 2816
A_GROUPS = 4
A_GROUP_DIM = 128
A_WIDTH = A_GROUPS * A_GROUP_DIM
SGU_CHUNK = 128
B_HEADS = 8
HEAD_DIM = 64
B_WIDTH = B_HEADS * HEAD_DIM
IN_WIDTH = 2 * A_WIDTH + 3 * B_WIDTH
DILATIONS = (1, 4, 16)
HALF_STEPS = 64
ROPE_THETA = 10000.0
POOL_WINDOWS = (2, 4, 8, 16)
POOL_GROUP_DIM = D_MODEL // len(POOL_WINDOWS)
RMS_EPS = 1e-6
NEG_BIG = -1e30

BF16 = jnp.bfloat16
F32 = jnp.float32

FFN_TM = 1024
FFN_SUB = 512
FFN_FC = 256
PROJ_TM = 256
POOL_HALO = 8
ATT_QB = 128
ATT_KW = ATT_QB + 2 * HALF_STEPS
ATT_GROUP = 16


def _params(n_grid_axes):
    return pltpu.CompilerParams(
        dimension_semantics=("arbitrary",) * n_grid_axes,
        vmem_limit_bytes=VMEM_LIMIT_BYTES)


def _resident(shape):
    nd = len(shape)
    return pl.BlockSpec(shape, lambda *_: (0,) * nd, pipeline_mode=pl.Buffered(1))


class _Layer(NamedTuple):
    stacked: jax.Array
    index: int


def _operand(w):
    if not isinstance(w, _Layer):
        return w, _resident(w.shape)
    shape = w.stacked.shape[1:]
    spec = pl.BlockSpec((None, *shape), lambda *_: (w.index,) + (0,) * len(shape),
                        pipeline_mode=pl.Buffered(1))
    return w.stacked, spec


def _row_vectors(stacked):
    return stacked.reshape(stacked.shape[0], 1, stacked.shape[1])


def _rms_rows(x, gain):
    ms = jnp.mean(x * x, axis=-1, keepdims=True)
    return x * lax.rsqrt(ms + RMS_EPS) * gain


def _swiglu_residual(x_of_rows, gain_ref, wg_ref, wu_ref, wd_ref, o_ref):
    for s in range(FFN_TM // FFN_SUB):
        rows = slice(s * FFN_SUB, (s + 1) * FFN_SUB)
        x = x_of_rows(rows)
        o_ref[rows, :] = x
        h = _rms_rows(x, gain_ref[...]).astype(BF16)
        for c in range(D_FF // FFN_FC):
            cols = slice(c * FFN_FC, (c + 1) * FFN_FC)
            g = jnp.dot(h, wg_ref[:, cols], preferred_element_type=F32)
            u = jnp.dot(h, wu_ref[:, cols], preferred_element_type=F32)
            a = (g * jax.nn.sigmoid(g) * u).astype(BF16)
            o_ref[rows, :] += jnp.dot(a, wd_ref[cols, :], preferred_element_type=F32)


def _ffn_body(x_ref, *ffn_refs):
    _swiglu_residual(lambda rows: x_ref[rows, :], *ffn_refs)


def _mix_ffn_body(x_ref, a_ref, b_ref, wo_ref, *ffn_refs):
    def mixed_rows(rows):
        return (x_ref[rows, :]
                + jnp.dot(a_ref[rows, :], wo_ref[0:A_WIDTH, :], preferred_element_type=F32)
                + jnp.dot(b_ref[rows, :], wo_ref[A_WIDTH:, :], preferred_element_type=F32))

    _swiglu_residual(mixed_rows, *ffn_refs)


def _pool_ffn_body(seq_len, x_ref, prev_ref, next_ref, mgain_ref, wp_ref, scale_ref,
                   gain_ref, wg_ref, wu_ref, wd_ref, o_ref, hext_ref):
    tiles_per_seq = seq_len // FFN_TM
    tile = pl.program_id(0) % tiles_per_seq
    mgain = mgain_ref[...]
    hext_ref[0:POOL_HALO, :] = _rms_rows(prev_ref[...], mgain) * jnp.where(tile > 0, 1.0, 0.0)
    hext_ref[POOL_HALO:POOL_HALO + FFN_TM, :] = _rms_rows(x_ref[...], mgain)
    hext_ref[POOL_HALO + FFN_TM:, :] = (_rms_rows(next_ref[...], mgain)
                                        * jnp.where(tile < tiles_per_seq - 1, 1.0, 0.0))

    def pooled_rows(rows):
        n_rows = rows.stop - rows.start
        first = POOL_HALO + rows.start
        pos = (tile * FFN_TM + rows.start
               + lax.broadcasted_iota(jnp.int32, (n_rows, 1), 0))
        groups = []
        for g, win in enumerate(POOL_WINDOWS):
            lo = win // 2
            hi = win - 1 - lo
            cols = slice(g * POOL_GROUP_DIM, (g + 1) * POOL_GROUP_DIM)
            wsum = sum(hext_ref[first + k:first + k + n_rows, cols] for k in range(-lo, hi + 1))
            cnt = jnp.minimum(pos + (hi + 1), seq_len) - jnp.maximum(pos - lo, 0)
            h = hext_ref[first:first + n_rows, cols]
            y = (wsum / cnt.astype(F32) - h).astype(BF16)
            mixed = jnp.dot(y, wp_ref[g], preferred_element_type=F32)
            groups.append(x_ref[rows, cols] + mixed * scale_ref[:, cols])
        return jnp.concatenate(groups, axis=1)

    _swiglu_residual(pooled_rows, gain_ref, wg_ref, wu_ref, wd_ref, o_ref)


def _ffn_call(body, name, x, ffn_params, pre_args=(), pre_specs=(), extra_scratch=()):
    t, d = x.shape
    row = pl.BlockSpec((FFN_TM, d), lambda i: (i, 0))
    arrays, specs = zip(*[_operand(p) for p in ffn_params])
    return pl.pallas_call(
        body,
        out_shape=jax.ShapeDtypeStruct((t, d), F32),
        grid=(t // FFN_TM,),
        in_specs=[row, *pre_specs, *specs],
        out_specs=row,
        scratch_shapes=list(extra_scratch),
        compiler_params=_params(1),
        name=name,
    )(x, *pre_args, *arrays)


def _ffn(x, ffn_params):
    return _ffn_call(_ffn_body, "ffn", x, ffn_params)


def _mix_ffn(x, a, b, w_out, ffn_params):
    row = lambda w: pl.BlockSpec((FFN_TM, w), lambda i: (i, 0))
    w_arr, w_spec = _operand(w_out)
    return _ffn_call(_mix_ffn_body, "even_out_ffn", x, ffn_params,
                     pre_args=(a, b, w_arr), pre_specs=(row(A_WIDTH), row(B_WIDTH), w_spec))


def _pool_ffn(x, seq_len, pool_params, ffn_params):
    t, d = x.shape
    per_tile = FFN_TM // POOL_HALO
    n_halo_blocks = t // POOL_HALO
    prev = pl.BlockSpec((POOL_HALO, d), lambda i: (jnp.maximum(i * per_tile - 1, 0), 0))
    nxt = pl.BlockSpec((POOL_HALO, d),
                       lambda i: (jnp.minimum((i + 1) * per_tile, n_halo_blocks - 1), 0))
    arrays, specs = zip(*[_operand(p) for p in pool_params])
    return _ffn_call(functools.partial(_pool_ffn_body, seq_len), "pool_ffn", x, ffn_params,
                     pre_args=(x, x, *arrays), pre_specs=(prev, nxt, *specs),
                     extra_scratch=(pltpu.VMEM((FFN_TM + 2 * POOL_HALO, d), F32),))


def _gelu(z):
    return 0.5 * z * (1.0 + lax.erf(z * (1.0 / math.sqrt(2.0))))


ROPE_HALF = HEAD_DIM // 2


def _qk_lane_map():
    lane = np.arange(LANES)
    return (lane // ROPE_HALF) % 2, (lane // HEAD_DIM) * ROPE_HALF + lane % ROPE_HALF


def _qk_head0_lanes():
    lane = lax.broadcasted_iota(jnp.int32, (1, LANES), 1)
    return (lane // ROPE_HALF) % 2 == 0


def _head_rms(t, gain, head0):
    sq = t * t
    ms_a = jnp.sum(jnp.where(head0, sq, 0.0), axis=-1, keepdims=True) * (1.0 / HEAD_DIM)
    ms_b = jnp.sum(jnp.where(head0, 0.0, sq), axis=-1, keepdims=True) * (1.0 / HEAD_DIM)
    r = jnp.where(head0, lax.rsqrt(ms_a + RMS_EPS), lax.rsqrt(ms_b + RMS_EPS))
    return t * r * gain


def _rope(t, cos, sin_signed):
    return t * cos + pltpu.roll(t, LANES // 2, axis=1) * sin_signed


def _proj_body(x_ref, gain_ref, win_ref, sgun_ref, ws_ref, bs_ref, qn_ref, kn_ref,
               cos_ref, sin_ref, a_ref, q_ref, k_ref, v_ref):
    h = _rms_rows(x_ref[...], gain_ref[...]).astype(BF16)

    def proj(lo, width):
        return jnp.dot(h, win_ref[:, lo:lo + width], preferred_element_type=F32)

    zq = proj(2 * A_WIDTH, B_WIDTH)
    zk = proj(2 * A_WIDTH + B_WIDTH, B_WIDTH)

    gv = _gelu(proj(A_WIDTH, A_WIDTH))
    u = _gelu(proj(0, A_WIDTH))
    n_chunks = PROJ_TM // SGU_CHUNK
    for g in range(A_GROUPS):
        cols = slice(g * A_GROUP_DIM, (g + 1) * A_GROUP_DIM)
        gvn = _rms_rows(gv[:, cols], sgun_ref[:, cols]).astype(BF16)
        rhs = jnp.concatenate(
            [gvn[c * SGU_CHUNK:(c + 1) * SGU_CHUNK, :] for c in range(n_chunks)], axis=1)
        mixed = jnp.dot(ws_ref[g], rhs, preferred_element_type=F32)
        for c in range(n_chunks):
            rows = slice(c * SGU_CHUNK, (c + 1) * SGU_CHUNK)
            m = mixed[:, c * A_GROUP_DIM:(c + 1) * A_GROUP_DIM] + bs_ref[g]
            a_ref[rows, cols] = (u[rows, cols] * m).astype(a_ref.dtype)

    cos, sin_signed = cos_ref[...], sin_ref[...]
    head0 = _qk_head0_lanes()
    scale = 1.0 / math.sqrt(HEAD_DIM)
    for j in range(B_WIDTH // LANES):
        cols = slice(j * LANES, (j + 1) * LANES)
        qn = _head_rms(zq[:, cols], qn_ref[:, cols], head0)
        kn = _head_rms(zk[:, cols], kn_ref[:, cols], head0)
        q_ref[:, cols] = (_rope(qn, cos, sin_signed) * scale).astype(q_ref.dtype)
        k_ref[:, cols] = _rope(kn, cos, sin_signed).astype(k_ref.dtype)
    v_ref[...] = proj(2 * A_WIDTH + 2 * B_WIDTH, B_WIDTH).astype(v_ref.dtype)


def _rope_tables(seq_len):
    pos = jnp.arange(seq_len, dtype=F32)
    inv_freq = ROPE_THETA ** (-jnp.arange(0, HEAD_DIM, 2, dtype=F32) / HEAD_DIM)
    ang = jnp.tile(pos[:, None] * inv_freq[None, :], (1, LANES // ROPE_HALF))
    first_half = jnp.asarray(_qk_lane_map()[1] < ROPE_HALF)
    sin = jnp.sin(ang)
    return jnp.cos(ang), jnp.where(first_half, -sin, sin)


def _to_qk_order(a):
    lead = a.shape[:-1]
    a = a.reshape(*lead, B_HEADS // 2, 2, 2, ROPE_HALF)
    return jnp.swapaxes(a, -3, -2).reshape(*lead, B_WIDTH)


def _even_proj(x, seq_len, proj_params):
    t, d = x.shape
    tiles_per_seq = seq_len // PROJ_TM
    cos, sin_signed = _rope_tables(seq_len)
    row = lambda w: pl.BlockSpec((PROJ_TM, w), lambda i: (i, 0))
    tab = pl.BlockSpec((PROJ_TM, LANES), lambda i: (i % tiles_per_seq, 0))
    arrays, specs = zip(*[_operand(p) for p in proj_params])
    return pl.pallas_call(
        _proj_body,
        out_shape=tuple(jax.ShapeDtypeStruct((t, w), BF16)
                        for w in (A_WIDTH, B_WIDTH, B_WIDTH, B_WIDTH)),
        grid=(t // PROJ_TM,),
        in_specs=[row(d), *specs, tab, tab],
        out_specs=(row(A_WIDTH), row(B_WIDTH), row(B_WIDTH), row(B_WIDTH)),
        compiler_params=_params(1),
        name="even_proj",
    )(x, *arrays, cos, sin_signed)


def _attn_body(seq_len, *refs):
    n_branches = len(DILATIONS)
    n_slabs = sum(DILATIONS)
    q_slabs, k_slabs, v_slabs = (refs[i * n_slabs:(i + 1) * n_slabs] for i in range(3))
    bias_ref, o_ref = refs[3 * n_slabs:3 * n_slabs + 2]
    num_ref, max_ref, den_ref = refs[3 * n_slabs + 2:]
    first_slab = [sum(DILATIONS[:b]) for b in range(n_branches)]

    head0_qk = _qk_head0_lanes()
    head0_v = lax.broadcasted_iota(jnp.int32, (1, LANES), 1) < HEAD_DIM

    def one_block(branch, r, n):
        dil = DILATIONS[branch]
        seg = seq_len // dil
        n_blocks = seg // ATT_QB
        slab = first_slab[branch] + r
        q_rows = pl.ds(pl.multiple_of(n * ATT_QB, ATT_QB), ATT_QB)
        k_start = jnp.clip(n * ATT_QB - HALF_STEPS, 0, seg - ATT_KW)
        k_rows = pl.ds(pl.multiple_of(k_start, HALF_STEPS), ATT_KW)
        q = q_slabs[slab][q_rows, :]
        kw = k_slabs[slab][k_rows, :]
        vw = v_slabs[slab][k_rows, :]
        zero = jnp.zeros_like(q)
        q2 = jnp.concatenate([jnp.where(head0_qk, q, zero), jnp.where(head0_qk, zero, q)], axis=0)
        s = lax.dot_general(q2, kw, (((1,), (1,)), ((), ())), preferred_element_type=F32)
        edge = jnp.where(n == 0, 1, jnp.where(n == n_blocks - 1, 2, 0))
        s = s + bias_ref[edge]
        m = jnp.max(s, axis=-1, keepdims=True)
        p = jnp.exp(s - m)
        l = jnp.sum(p, axis=-1, keepdims=True)
        o2 = jnp.dot(p.astype(BF16), vw, preferred_element_type=F32)
        nums = [jnp.where(head0_v, o2[:ATT_QB], o2[ATT_QB:])]
        maxima = [jnp.where(head0_v, m[:ATT_QB], m[ATT_QB:])]
        dens = [jnp.where(head0_v, l[:ATT_QB], l[ATT_QB:])]
        if branch > 0:
            out_rows = pl.ds(n * (ATT_QB * dil) + r, ATT_QB, stride=dil)
            num_ref[branch - 1, out_rows, :] = nums[0]
            max_ref[branch - 1, out_rows, :] = maxima[0]
            den_ref[branch - 1, out_rows, :] = dens[0]
            return
        for b in range(n_branches - 1):
            nums.append(num_ref[b, q_rows, :])
            maxima.append(max_ref[b, q_rows, :])
            dens.append(den_ref[b, q_rows, :])
        m_all = functools.reduce(jnp.maximum, maxima)
        weights = [jnp.exp(m_b - m_all) for m_b in maxima]
        num = sum(w * n_b for w, n_b in zip(weights, nums))
        den = sum(w * d_b for w, d_b in zip(weights, dens))
        o_ref[q_rows, :] = (num / den).astype(o_ref.dtype)

    assert DILATIONS[0] == 1
    for branch in reversed(range(n_branches)):
        dil = DILATIONS[branch]
        per_residue = ATT_GROUP // dil

        def group(t, carry, branch=branch, dil=dil, per_residue=per_residue):
            for r in range(dil):
                for i in range(per_residue):
                    one_block(branch, r, t * per_residue + i)
            return carry

        lax.fori_loop(0, seq_len // ATT_QB // ATT_GROUP, group, 0)


def _attn_bias():
    qi = np.arange(2 * ATT_QB)[:, None] % ATT_QB
    kj = np.arange(ATT_KW)[None, :]
    variants = [np.abs(kj - lead - qi) <= HALF_STEPS for lead in (HALF_STEPS, 0, 2 * HALF_STEPS)]
    return jnp.asarray(np.stack([np.where(v, 0.0, NEG_BIG) for v in variants]), F32)


def _attention(q, k, v, batch, seq_len):
    t, w = q.shape
    bias = _attn_bias()
    n_pairs = w // LANES
    specs, args = [], []
    for tensor in (q, k, v):
        for dil in DILATIONS:
            view = tensor.reshape(batch, seq_len // dil, dil * w)
            for r in range(dil):
                specs.append(pl.BlockSpec((None, seq_len // dil, LANES),
                                          lambda b, j, r=r: (b, 0, r * n_pairs + j)))
                args.append(view)
    state = pltpu.VMEM((len(DILATIONS) - 1, seq_len, LANES), F32)
    out = pl.pallas_call(
        functools.partial(_attn_body, seq_len),
        out_shape=jax.ShapeDtypeStruct((batch, seq_len, w), BF16),
        grid=(batch, n_pairs),
        in_specs=[*specs, _resident(bias.shape)],
        out_specs=pl.BlockSpec((None, seq_len, LANES), lambda b, j: (b, 0, j)),
        scratch_shapes=[state, state, state],
        compiler_params=_params(2),
        name="dilated_attn",
    )(*args, bias)
    return out.reshape(t, w)


def kernel(x, ffn1_norm, ffn1_w_gate, ffn1_w_up, ffn1_w_down, mix_norm, even_w_in, sgu_norm,
           sgu_w_spatial, sgu_b_spatial, attn_q_norm, attn_k_norm, even_w_out, pool_w_group,
           pool_scale, ffn2_norm, ffn2_w_gate, ffn2_w_up, ffn2_w_down):
    batch, seq_len, d = x.shape
    depth = ffn1_norm.shape[0]
    assert seq_len % (ATT_QB * ATT_GROUP) == 0 and seq_len // max(DILATIONS) >= ATT_KW
    assert all(ATT_GROUP % dil == 0 for dil in DILATIONS)
    assert seq_len % FFN_TM == 0 and seq_len % PROJ_TM == 0 and d == D_MODEL
    cast = lambda w: w.astype(BF16)
    ffn1 = (_row_vectors(ffn1_norm), cast(ffn1_w_gate), cast(ffn1_w_up), cast(0.5 * ffn1_w_down))
    ffn2 = (_row_vectors(ffn2_norm), cast(ffn2_w_gate), cast(ffn2_w_up), cast(0.5 * ffn2_w_down))
    mix_gain = _row_vectors(mix_norm)
    n_even = even_w_in.shape[0]
    q_lo, k_lo, v_lo = 2 * A_WIDTH, 2 * A_WIDTH + B_WIDTH, 2 * A_WIDTH + 2 * B_WIDTH
    w_in = jnp.concatenate([even_w_in[..., :q_lo], _to_qk_order(even_w_in[..., q_lo:k_lo]),
                            _to_qk_order(even_w_in[..., k_lo:v_lo]), even_w_in[..., v_lo:]],
                           axis=-1)
    proj = (cast(w_in), sgu_norm.reshape(n_even, 1, A_WIDTH), cast(sgu_w_spatial),
            jnp.broadcast_to(sgu_b_spatial[..., None], sgu_b_spatial.shape + (A_GROUP_DIM,)),
            _row_vectors(_to_qk_order(jnp.tile(attn_q_norm, (1, B_HEADS)))),
            _row_vectors(_to_qk_order(jnp.tile(attn_k_norm, (1, B_HEADS)))))
    w_out = cast(even_w_out)
    pool = (cast(pool_w_group), _row_vectors(pool_scale))

    pick = lambda stacked, index: [_Layer(p, index) for p in stacked]
    xf = x.reshape(batch * seq_len, d)
    for layer in range(depth):
        xf = _ffn(xf, pick(ffn1, layer))
        j = layer // 2
        if layer % 2 == 0:
            a, q, k, v = _even_proj(xf, seq_len, [_Layer(mix_gain, layer), *pick(proj, j)])
            b = _attention(q, k, v, batch, seq_len)
            xf = _mix_ffn(xf, a, b, _Layer(w_out, j), pick(ffn2, layer))
        else:
            xf = _pool_ffn(xf, seq_len, [_Layer(mix_gain, layer), *pick(pool, j)],
                           pick(ffn2, layer))
    return xf.reshape(batch, seq_len, d)
```

```python
import functools
import math
from typing import NamedTuple

import jax
import jax.numpy as jnp
import numpy as np
from jax import lax
from jax.experimental import pallas as pl
from jax.experimental.pallas import tpu as pltpu

LANES = 128
SUBLANES = 8
VMEM_LIMIT_BYTES = 56 * 1024 * 1024

D_MODEL = 1024
D_FF = 2816
A_GROUPS = 4
A_GROUP_DIM = 128
A_WIDTH = A_GROUPS * A_GROUP_DIM
SGU_CHUNK = 128
B_HEADS = 8
HEAD_DIM = 64
B_WIDTH = B_HEADS * HEAD_DIM
IN_WIDTH = 2 * A_WIDTH + 3 * B_WIDTH
DILATIONS = (1, 4, 16)
HALF_STEPS = 64
ROPE_THETA = 10000.0
POOL_WINDOWS = (2, 4, 8, 16)
POOL_GROUP_DIM = D_MODEL // len(POOL_WINDOWS)
RMS_EPS = 1e-6
NEG_BIG = -1e30

BF16 = jnp.bfloat16
F32 = jnp.float32

FFN_TM = 1024
FFN_SUB = 512
FFN_FC = 256
PROJ_TM = 256
POOL_HALO = 8
ATT_QB = 128
ATT_KW = ATT_QB + 2 * HALF_STEPS
ATT_GROUP = 16


def _params(n_grid_axes):
    return pltpu.CompilerParams(
        dimension_semantics=("arbitrary",) * n_grid_axes,
        vmem_limit_bytes=VMEM_LIMIT_BYTES)


def _resident(shape):
    nd = len(shape)
    return pl.BlockSpec(shape, lambda *_: (0,) * nd, pipeline_mode=pl.Buffered(1))


class _Layer(NamedTuple):
    stacked: jax.Array
    index: int


def _operand(w):
    if not isinstance(w, _Layer):
        return w, _resident(w.shape)
    shape = w.stacked.shape[1:]
    spec = pl.BlockSpec((None, *shape), lambda *_: (w.index,) + (0,) * len(shape),
                        pipeline_mode=pl.Buffered(1))
    return w.stacked, spec


def _row_vectors(stacked):
    return stacked.reshape(stacked.shape[0], 1, stacked.shape[1])


def _rms_rows(x, gain):
    ms = jnp.mean(x * x, axis=-1, keepdims=True)
    return x * lax.rsqrt(ms + RMS_EPS) * gain


def _swiglu_residual(x_of_rows, gain_ref, wg_ref, wu_ref, wd_ref, o_ref):
    for s in range(FFN_TM // FFN_SUB):
        rows = slice(s * FFN_SUB, (s + 1) * FFN_SUB)
        x = x_of_rows(rows)
        o_ref[rows, :] = x
        h = _rms_rows(x, gain_ref[...]).astype(BF16)
        for c in range(D_FF // FFN_FC):
            cols = slice(c * FFN_FC, (c + 1) * FFN_FC)
            g = jnp.dot(h, wg_ref[:, cols], preferred_element_type=F32)
            u = jnp.dot(h, wu_ref[:, cols], preferred_element_type=F32)
            a = (g * jax.nn.sigmoid(g) * u).astype(BF16)
            o_ref[rows, :] += jnp.dot(a, wd_ref[cols, :], preferred_element_type=F32)


def _ffn_body(x_ref, *ffn_refs):
    _swiglu_residual(lambda rows: x_ref[rows, :], *ffn_refs)


def _mix_ffn_body(x_ref, a_ref, b_ref, wo_ref, *ffn_refs):
    def mixed_rows(rows):
        return (x_ref[rows, :]
                + jnp.dot(a_ref[rows, :], wo_ref[0:A_WIDTH, :], preferred_element_type=F32)
                + jnp.dot(b_ref[rows, :], wo_ref[A_WIDTH:, :], preferred_element_type=F32))

    _swiglu_residual(mixed_rows, *ffn_refs)


def _pool_ffn_body(seq_len, x_ref, prev_ref, next_ref, mgain_ref, wp_ref, scale_ref,
                   gain_ref, wg_ref, wu_ref, wd_ref, o_ref, hext_ref):
    tiles_per_seq = seq_len // FFN_TM
    tile = pl.program_id(0) % tiles_per_seq
    mgain = mgain_ref[...]
    hext_ref[0:POOL_HALO, :] = _rms_rows(prev_ref[...], mgain) * jnp.where(tile > 0, 1.0, 0.0)
    hext_ref[POOL_HALO:POOL_HALO + FFN_TM, :] = _rms_rows(x_ref[...], mgain)
    hext_ref[POOL_HALO + FFN_TM:, :] = (_rms_rows(next_ref[...], mgain)
                                        * jnp.where(tile < tiles_per_seq - 1, 1.0, 0.0))

    def pooled_rows(rows):
        n_rows = rows.stop - rows.start
        first = POOL_HALO + rows.start
        pos = (tile * FFN_TM + rows.start
               + lax.broadcasted_iota(jnp.int32, (n_rows, 1), 0))
        groups = []
        for g, win in enumerate(POOL_WINDOWS):
            lo = win // 2
            hi = win - 1 - lo
            cols = slice(g * POOL_GROUP_DIM, (g + 1) * POOL_GROUP_DIM)
            wsum = sum(hext_ref[first + k:first + k + n_rows, cols] for k in range(-lo, hi + 1))
            cnt = jnp.minimum(pos + (hi + 1), seq_len) - jnp.maximum(pos - lo, 0)
            h = hext_ref[first:first + n_rows, cols]
            y = (wsum / cnt.astype(F32) - h).astype(BF16)
            mixed = jnp.dot(y, wp_ref[g], preferred_element_type=F32)
            groups.append(x_ref[rows, cols] + mixed * scale_ref[:, cols])
        return jnp.concatenate(groups, axis=1)

    _swiglu_residual(pooled_rows, gain_ref, wg_ref, wu_ref, wd_ref, o_ref)


def _ffn_call(body, name, x, ffn_params, pre_args=(), pre_specs=(), extra_scratch=()):
    t, d = x.shape
    row = pl.BlockSpec((FFN_TM, d), lambda i: (i, 0))
    arrays, specs = zip(*[_operand(p) for p in ffn_params])
    return pl.pallas_call(
        body,
        out_shape=jax.ShapeDtypeStruct((t, d), F32),
        grid=(t // FFN_TM,),
        in_specs=[row, *pre_specs, *specs],
        out_specs=row,
        scratch_shapes=list(extra_scratch),
        compiler_params=_params(1),
        name=name,
    )(x, *pre_args, *arrays)


def _ffn(x, ffn_params):
    return _ffn_call(_ffn_body, "ffn", x, ffn_params)


def _mix_ffn(x, a, b, w_out, ffn_params):
    row = lambda w: pl.BlockSpec((FFN_TM, w), lambda i: (i, 0))
    w_arr, w_spec = _operand(w_out)
    return _ffn_call(_mix_ffn_body, "even_out_ffn", x, ffn_params,
                     pre_args=(a, b, w_arr), pre_specs=(row(A_WIDTH), row(B_WIDTH), w_spec))


def _pool_ffn(x, seq_len, pool_params, ffn_params):
    t, d = x.shape
    per_tile = FFN_TM // POOL_HALO
    n_halo_blocks = t // POOL_HALO
    prev = pl.BlockSpec((POOL_HALO, d), lambda i: (jnp.maximum(i * per_tile - 1, 0), 0))
    nxt = pl.BlockSpec((POOL_HALO, d),
                       lambda i: (jnp.minimum((i + 1) * per_tile, n_halo_blocks - 1), 0))
    arrays, specs = zip(*[_operand(p) for p in pool_params])
    return _ffn_call(functools.partial(_pool_ffn_body, seq_len), "pool_ffn", x, ffn_params,
                     pre_args=(x, x, *arrays), pre_specs=(prev, nxt, *specs),
                     extra_scratch=(pltpu.VMEM((FFN_TM + 2 * POOL_HALO, d), F32),))


def _gelu(z):
    return 0.5 * z * (1.0 + lax.erf(z * (1.0 / math.sqrt(2.0))))


ROPE_HALF = HEAD_DIM // 2


def _qk_lane_map():
    lane = np.arange(LANES)
    return (lane // ROPE_HALF) % 2, (lane // HEAD_DIM) * ROPE_HALF + lane % ROPE_HALF


def _qk_head0_lanes():
    lane = lax.broadcasted_iota(jnp.int32, (1, LANES), 1)
    return (lane // ROPE_HALF) % 2 == 0


def _head_rms(t, gain, head0):
    sq = t * t
    ms_a = jnp.sum(jnp.where(head0, sq, 0.0), axis=-1, keepdims=True) * (1.0 / HEAD_DIM)
    ms_b = jnp.sum(jnp.where(head0, 0.0, sq), axis=-1, keepdims=True) * (1.0 / HEAD_DIM)
    r = jnp.where(head0, lax.rsqrt(ms_a + RMS_EPS), lax.rsqrt(ms_b + RMS_EPS))
    return t * r * gain


def _rope(t, cos, sin_signed):
    return t * cos + pltpu.roll(t, LANES // 2, axis=1) * sin_signed


def _proj_body(x_ref, gain_ref, win_ref, sgun_ref, ws_ref, bs_ref, qn_ref, kn_ref,
               cos_ref, sin_ref, a_ref, q_ref, k_ref, v_ref):
    h = _rms_rows(x_ref[...], gain_ref[...]).astype(BF16)

    def proj(lo, width):
        return jnp.dot(h, win_ref[:, lo:lo + width], preferred_element_type=F32)

    zq = proj(2 * A_WIDTH, B_WIDTH)
    zk = proj(2 * A_WIDTH + B_WIDTH, B_WIDTH)

    gv = _gelu(proj(A_WIDTH, A_WIDTH))
    u = _gelu(proj(0, A_WIDTH))
    n_chunks = PROJ_TM // SGU_CHUNK
    for g in range(A_GROUPS):
        cols = slice(g * A_GROUP_DIM, (g + 1) * A_GROUP_DIM)
        gvn = _rms_rows(gv[:, cols], sgun_ref[:, cols]).astype(BF16)
        rhs = jnp.concatenate(
            [gvn[c * SGU_CHUNK:(c + 1) * SGU_CHUNK, :] for c in range(n_chunks)], axis=1)
        mixed = jnp.dot(ws_ref[g], rhs, preferred_element_type=F32)
        for c in range(n_chunks):
            rows = slice(c * SGU_CHUNK, (c + 1) * SGU_CHUNK)
            m = mixed[:, c * A_GROUP_DIM:(c + 1) * A_GROUP_DIM] + bs_ref[g]
            a_ref[rows, cols] = (u[rows, cols] * m).astype(a_ref.dtype)

    cos, sin_signed = cos_ref[...], sin_ref[...]
    head0 = _qk_head0_lanes()
    scale = 1.0 / math.sqrt(HEAD_DIM)
    for j in range(B_WIDTH // LANES):
        cols = slice(j * LANES, (j + 1) * LANES)
        qn = _head_rms(zq[:, cols], qn_ref[:, cols], head0)
        kn = _head_rms(zk[:, cols], kn_ref[:, cols], head0)
        q_ref[:, cols] = (_rope(qn, cos, sin_signed) * scale).astype(q_ref.dtype)
        k_ref[:, cols] = _rope(kn, cos, sin_signed).astype(k_ref.dtype)
    v_ref[...] = proj(2 * A_WIDTH + 2 * B_WIDTH, B_WIDTH).astype(v_ref.dtype)


def _rope_tables(seq_len):
    pos = jnp.arange(seq_len, dtype=F32)
    inv_freq = ROPE_THETA ** (-jnp.arange(0, HEAD_DIM, 2, dtype=F32) / HEAD_DIM)
    ang = jnp.tile(pos[:, None] * inv_freq[None, :], (1, LANES // ROPE_HALF))
    first_half = jnp.asarray(_qk_lane_map()[1] < ROPE_HALF)
    sin = jnp.sin(ang)
    return jnp.cos(ang), jnp.where(first_half, -sin, sin)


def _to_qk_order(a):
    lead = a.shape[:-1]
    a = a.reshape(*lead, B_HEADS // 2, 2, 2, ROPE_HALF)
    return jnp.swapaxes(a, -3, -2).reshape(*lead, B_WIDTH)


def _even_proj(x, seq_len, proj_params):
    t, d = x.shape
    tiles_per_seq = seq_len // PROJ_TM
    cos, sin_signed = _rope_tables(seq_len)
    row = lambda w: pl.BlockSpec((PROJ_TM, w), lambda i: (i, 0))
    tab = pl.BlockSpec((PROJ_TM, LANES), lambda i: (i % tiles_per_seq, 0))
    arrays, specs = zip(*[_operand(p) for p in proj_params])
    return pl.pallas_call(
        _proj_body,
        out_shape=(jax.ShapeDtypeStruct((t, A_WIDTH), BF16),
                   *[jax.ShapeDtypeStruct((t, B_WIDTH), F32)] * 3),
        grid=(t // PROJ_TM,),
        in_specs=[row(d), *specs, tab, tab],
        out_specs=(row(A_WIDTH), row(B_WIDTH), row(B_WIDTH), row(B_WIDTH)),
        compiler_params=_params(1),
        name="even_proj",
    )(x, *arrays, cos, sin_signed)


def _attn_body(seq_len, q_ref, k_ref, v_ref, bias_ref, o_ref, *scratch):
    n_branches = len(DILATIONS)
    kv_res = [scratch[2 * i:2 * i + 2] for i in range(n_branches)]
    kv_stage = scratch[2 * n_branches:2 * n_branches + 2]
    num_ref, max_ref, den_ref = scratch[2 * n_branches + 2:]

    d1 = DILATIONS[1]
    assert DILATIONS == (1, d1, d1 * d1)
    seg1, seg2 = seq_len // d1, seq_len // (d1 * d1)
    zeros = jnp.zeros((HALF_STEPS, LANES), BF16)

    def put_segment(dst, index, rows_bf16):
        seg = rows_bf16.shape[0]
        base = index * (seg + 2 * HALF_STEPS)
        dst[base:base + HALF_STEPS, :] = zeros
        dst[base + HALF_STEPS:base + HALF_STEPS + seg, :] = rows_bf16
        dst[base + HALF_STEPS + seg:base + seg + 2 * HALF_STEPS, :] = zeros

    for which, (src, stage) in enumerate(zip((k_ref, v_ref), kv_stage)):
        put_segment(kv_res[0][which], 0, src[...].astype(BF16))
        for r in range(d1):
            stage[r * seg1:(r + 1) * seg1, :] = src[pl.ds(r, seg1, stride=d1), :]
        for r in range(d1):
            put_segment(kv_res[1][which], r, stage[r * seg1:(r + 1) * seg1, :].astype(BF16))
            for hop in range(d1):
                rows = stage[pl.ds(r * seg1 + hop, seg2, stride=d1), :]
                put_segment(kv_res[2][which], r + d1 * hop, rows.astype(BF16))

    head0_qk = _qk_head0_lanes()
    head0_v = lax.broadcasted_iota(jnp.int32, (1, LANES), 1) < HEAD_DIM

    def one_block(branch, dil, n_blocks, flat):
        r = flat // n_blocks
        n = flat % n_blocks
        q_start = n * (ATT_QB * dil) + r
        if dil == 1:
            q_rows = pl.ds(pl.multiple_of(q_start, ATT_QB), ATT_QB)
        else:
            q_rows = pl.ds(q_start, ATT_QB, stride=dil)
        pitch = seq_len // dil + 2 * HALF_STEPS
        k_rows = pl.ds(pl.multiple_of(r * pitch + n * ATT_QB, ATT_QB), ATT_KW)
        kres, vres = kv_res[branch]
        q = q_ref[q_rows, :]
        kw = kres[k_rows, :]
        vw = vres[k_rows, :]
        q2 = jnp.concatenate([jnp.where(head0_qk, q, 0.0), jnp.where(head0_qk, 0.0, q)],
                             axis=0).astype(BF16)
        s = lax.dot_general(q2, kw, (((1,), (1,)), ((), ())), preferred_element_type=F32)
        edge = jnp.where(n == 0, 1, 0) + jnp.where(n == n_blocks - 1, 2, 0)
        s = s + bias_ref[edge]
        m = jnp.max(s, axis=-1, keepdims=True)
        p = jnp.exp(s - m)
        l = jnp.sum(p, axis=-1, keepdims=True)
        o2 = jnp.dot(p.astype(BF16), vw, preferred_element_type=F32)
        nums = [jnp.where(head0_v, o2[:ATT_QB], o2[ATT_QB:])]
        maxima = [jnp.where(head0_v, m[:ATT_QB], m[ATT_QB:])]
        dens = [jnp.where(head0_v, l[:ATT_QB], l[ATT_QB:])]
        if branch > 0:
            num_ref[branch - 1, q_rows, :] = nums[0]
            max_ref[branch - 1, q_rows, :] = maxima[0]
            den_ref[branch - 1, q_rows, :] = dens[0]
            return
        for b in range(n_branches - 1):
            nums.append(num_ref[b, q_rows, :])
            maxima.append(max_ref[b, q_rows, :])
            dens.append(den_ref[b, q_rows, :])
        m_all = functools.reduce(jnp.maximum, maxima)
        weights = [jnp.exp(m_b - m_all) for m_b in maxima]
        num = sum(w * n_b for w, n_b in zip(weights, nums))
        den = sum(w * d_b for w, d_b in zip(weights, dens))
        o_ref[q_rows, :] = (num / den).astype(o_ref.dtype)

    assert DILATIONS[0] == 1
    for branch in reversed(range(n_branches)):
        dil = DILATIONS[branch]
        n_blocks = seq_len // dil // ATT_QB

        def group(t, carry, branch=branch, dil=dil, n_blocks=n_blocks):
            for g in range(ATT_GROUP):
                one_block(branch, dil, n_blocks, t * ATT_GROUP + g)
            return carry

        lax.fori_loop(0, seq_len // ATT_QB // ATT_GROUP, group, 0)


def _attn_bias():
    qi = np.arange(2 * ATT_QB)[:, None] % ATT_QB
    kj = np.arange(ATT_KW)[None, :]
    band = np.abs(kj - HALF_STEPS - qi) <= HALF_STEPS
    first = kj >= HALF_STEPS
    last = kj < ATT_KW - HALF_STEPS
    variants = (band, band & first, band & last, band & first & last)
    return jnp.asarray(np.stack([np.where(v, 0.0, NEG_BIG) for v in variants]), F32)


def _attention(q, k, v, batch, seq_len):
    t, w = q.shape
    bias = _attn_bias()
    shape3 = (batch, seq_len, w)
    slab = pl.BlockSpec((None, seq_len, LANES), lambda b, j: (b, 0, j))
    kv_scratch = [pltpu.VMEM((seq_len + 2 * HALF_STEPS * dil, LANES), BF16)
                  for dil in DILATIONS for _ in ("k", "v")]
    stage = pltpu.VMEM((seq_len, LANES), F32)
    state = pltpu.VMEM((len(DILATIONS) - 1, seq_len, LANES), F32)
    out = pl.pallas_call(
        functools.partial(_attn_body, seq_len),
        out_shape=jax.ShapeDtypeStruct(shape3, BF16),
        grid=(batch, w // LANES),
        in_specs=[slab, slab, slab, _resident(bias.shape)],
        out_specs=slab,
        scratch_shapes=[*kv_scratch, stage, stage, state, state, state],
        compiler_params=_params(2),
        name="dilated_attn",
    )(q.reshape(shape3), k.reshape(shape3), v.reshape(shape3), bias)
    return out.reshape(t, w)


def kernel(x, ffn1_norm, ffn1_w_gate, ffn1_w_up, ffn1_w_down, mix_norm, even_w_in, sgu_norm,
           sgu_w_spatial, sgu_b_spatial, attn_q_norm, attn_k_norm, even_w_out, pool_w_group,
           pool_scale, ffn2_norm, ffn2_w_gate, ffn2_w_up, ffn2_w_down):
    batch, seq_len, d = x.shape
    depth = ffn1_norm.shape[0]
    assert seq_len % (ATT_QB * max(DILATIONS)) == 0 and seq_len % (ATT_QB * ATT_GROUP) == 0
    assert seq_len % FFN_TM == 0 and seq_len % PROJ_TM == 0 and d == D_MODEL
    cast = lambda w: w.astype(BF16)
    ffn1 = (_row_vectors(ffn1_norm), cast(ffn1_w_gate), cast(ffn1_w_up), cast(0.5 * ffn1_w_down))
    ffn2 = (_row_vectors(ffn2_norm), cast(ffn2_w_gate), cast(ffn2_w_up), cast(0.5 * ffn2_w_down))
    mix_gain = _row_vectors(mix_norm)
    n_even = even_w_in.shape[0]
    q_lo, k_lo, v_lo = 2 * A_WIDTH, 2 * A_WIDTH + B_WIDTH, 2 * A_WIDTH + 2 * B_WIDTH
    w_in = jnp.concatenate([even_w_in[..., :q_lo], _to_qk_order(even_w_in[..., q_lo:k_lo]),
                            _to_qk_order(even_w_in[..., k_lo:v_lo]), even_w_in[..., v_lo:]],
                           axis=-1)
    proj = (cast(w_in), sgu_norm.reshape(n_even, 1, A_WIDTH), cast(sgu_w_spatial),
            jnp.broadcast_to(sgu_b_spatial[..., None], sgu_b_spatial.shape + (A_GROUP_DIM,)),
            _row_vectors(_to_qk_order(jnp.tile(attn_q_norm, (1, B_HEADS)))),
            _row_vectors(_to_qk_order(jnp.tile(attn_k_norm, (1, B_HEADS)))))
    w_out = cast(even_w_out)
    pool = (cast(pool_w_group), _row_vectors(pool_scale))

    pick = lambda stacked, index: [_Layer(p, index) for p in stacked]
    xf = x.reshape(batch * seq_len, d)
    for layer in range(depth):
        xf = _ffn(xf, pick(ffn1, layer))
        j = layer // 2
        if layer % 2 == 0:
            a, q, k, v = _even_proj(xf, seq_len, [_Layer(mix_gain, layer), *pick(proj, j)])
            b = _attention(q, k, v, batch, seq_len)
            xf = _mix_ffn(xf, a, b, _Layer(w_out, j), pick(ffn2, layer))
        else:
            xf = _pool_ffn(xf, seq_len, [_Layer(mix_gain, layer), *pick(pool, j)],
                           pick(ffn2, layer))
    return xf.reshape(batch, seq_len, d)
```

```python
import functools
import math
from typing import NamedTuple

import jax
import jax.numpy as jnp
import numpy as np
from jax import lax
from jax.experimental import pallas as pl
from jax.experimental.pallas import tpu as pltpu

LANES = 128
SUBLANES = 8
VMEM_LIMIT_BYTES = 56 * 1024 * 1024

D_MODEL = 1024
D_FF = 2816
A_GROUPS = 4
A_GROUP_DIM = 128
A_WIDTH = A_GROUPS * A_GROUP_DIM
SGU_CHUNK = 128
B_HEADS = 8
HEAD_DIM = 64
B_WIDTH = B_HEADS * HEAD_DIM
IN_WIDTH = 2 * A_WIDTH + 3 * B_WIDTH
DILATIONS = (1, 4, 16)
HALF_STEPS = 64
ROPE_THETA = 10000.0
POOL_WINDOWS = (2, 4, 8, 16)
POOL_GROUP_DIM = D_MODEL // len(POOL_WINDOWS)
RMS_EPS = 1e-6
NEG_BIG = -1e30

BF16 = jnp.bfloat16
F32 = jnp.float32

FFN_TM = 1024
FFN_SUB = 512
FFN_FC = 256
PROJ_TM = 256
POOL_HALO = 8
ATT_QB = 128
ATT_KW = ATT_QB + 2 * HALF_STEPS
ATT_GROUP = 32


def _params(n_grid_axes):
    return pltpu.CompilerParams(
        dimension_semantics=("arbitrary",) * n_grid_axes,
        vmem_limit_bytes=VMEM_LIMIT_BYTES)


def _resident(shape):
    nd = len(shape)
    return pl.BlockSpec(shape, lambda *_: (0,) * nd, pipeline_mode=pl.Buffered(1))


class _Layer(NamedTuple):
    stacked: jax.Array
    index: int


def _operand(w):
    if not isinstance(w, _Layer):
        return w, _resident(w.shape)
    shape = w.stacked.shape[1:]
    spec = pl.BlockSpec((None, *shape), lambda *_: (w.index,) + (0,) * len(shape),
                        pipeline_mode=pl.Buffered(1))
    return w.stacked, spec


def _row_vectors(stacked):
    return stacked.reshape(stacked.shape[0], 1, stacked.shape[1])


def _rms_rows(x, gain):
    ms = jnp.mean(x * x, axis=-1, keepdims=True)
    return x * lax.rsqrt(ms + RMS_EPS) * gain


def _swiglu_residual(x_of_rows, gain_ref, wg_ref, wu_ref, wd_ref, o_ref):
    for s in range(FFN_TM // FFN_SUB):
        rows = slice(s * FFN_SUB, (s + 1) * FFN_SUB)
        x = x_of_rows(rows)
        o_ref[rows, :] = x
        h = _rms_rows(x, gain_ref[...]).astype(BF16)
        for c in range(D_FF // FFN_FC):
            cols = slice(c * FFN_FC, (c + 1) * FFN_FC)
            g = jnp.dot(h, wg_ref[:, cols], preferred_element_type=F32)
            u = jnp.dot(h, wu_ref[:, cols], preferred_element_type=F32)
            a = (g * jax.nn.sigmoid(g) * u).astype(BF16)
            o_ref[rows, :] += jnp.dot(a, wd_ref[cols, :], preferred_element_type=F32)


def _ffn_body(x_ref, *ffn_refs):
    _swiglu_residual(lambda rows: x_ref[rows, :], *ffn_refs)


def _mix_ffn_body(x_ref, a_ref, b_ref, wo_ref, *ffn_refs):
    def mixed_rows(rows):
        return (x_ref[rows, :]
                + jnp.dot(a_ref[rows, :], wo_ref[0:A_WIDTH, :], preferred_element_type=F32)
                + jnp.dot(b_ref[rows, :], wo_ref[A_WIDTH:, :], preferred_element_type=F32))

    _swiglu_residual(mixed_rows, *ffn_refs)


def _pool_ffn_body(seq_len, x_ref, prev_ref, next_ref, mgain_ref, wp_ref, scale_ref,
                   gain_ref, wg_ref, wu_ref, wd_ref, o_ref, hext_ref):
    tiles_per_seq = seq_len // FFN_TM
    tile = pl.program_id(0) % tiles_per_seq
    mgain = mgain_ref[...]
    hext_ref[0:POOL_HALO, :] = _rms_rows(prev_ref[...], mgain) * jnp.where(tile > 0, 1.0, 0.0)
    hext_ref[POOL_HALO:POOL_HALO + FFN_TM, :] = _rms_rows(x_ref[...], mgain)
    hext_ref[POOL_HALO + FFN_TM:, :] = (_rms_rows(next_ref[...], mgain)
                                        * jnp.where(tile < tiles_per_seq - 1, 1.0, 0.0))

    def pooled_rows(rows):
        n_rows = rows.stop - rows.start
        first = POOL_HALO + rows.start
        pos = (tile * FFN_TM + rows.start
               + lax.broadcasted_iota(jnp.int32, (n_rows, 1), 0))
        groups = []
        for g, win in enumerate(POOL_WINDOWS):
            lo = win // 2
            hi = win - 1 - lo
            cols = slice(g * POOL_GROUP_DIM, (g + 1) * POOL_GROUP_DIM)
            wsum = sum(hext_ref[first + k:first + k + n_rows, cols] for k in range(-lo, hi + 1))
            cnt = jnp.minimum(pos + (hi + 1), seq_len) - jnp.maximum(pos - lo, 0)
            h = hext_ref[first:first + n_rows, cols]
            y = (wsum / cnt.astype(F32) - h).astype(BF16)
            mixed = jnp.dot(y, wp_ref[g], preferred_element_type=F32)
            groups.append(x_ref[rows, cols] + mixed * scale_ref[:, cols])
        return jnp.concatenate(groups, axis=1)

    _swiglu_residual(pooled_rows, gain_ref, wg_ref, wu_ref, wd_ref, o_ref)


def _ffn_call(body, name, x, ffn_params, pre_args=(), pre_specs=(), extra_scratch=()):
    t, d = x.shape
    row = pl.BlockSpec((FFN_TM, d), lambda i: (i, 0))
    arrays, specs = zip(*[_operand(p) for p in ffn_params])
    return pl.pallas_call(
        body,
        out_shape=jax.ShapeDtypeStruct((t, d), F32),
        grid=(t // FFN_TM,),
        in_specs=[row, *pre_specs, *specs],
        out_specs=row,
        scratch_shapes=list(extra_scratch),
        compiler_params=_params(1),
        name=name,
    )(x, *pre_args, *arrays)


def _ffn(x, ffn_params):
    return _ffn_call(_ffn_body, "ffn", x, ffn_params)


def _mix_ffn(x, a, b, w_out, ffn_params):
    row = lambda w: pl.BlockSpec((FFN_TM, w), lambda i: (i, 0))
    w_arr, w_spec = _operand(w_out)
    return _ffn_call(_mix_ffn_body, "even_out_ffn", x, ffn_params,
                     pre_args=(a, b, w_arr), pre_specs=(row(A_WIDTH), row(B_WIDTH), w_spec))


def _pool_ffn(x, seq_len, pool_params, ffn_params):
    t, d = x.shape
    per_tile = FFN_TM // POOL_HALO
    n_halo_blocks = t // POOL_HALO
    prev = pl.BlockSpec((POOL_HALO, d), lambda i: (jnp.maximum(i * per_tile - 1, 0), 0))
    nxt = pl.BlockSpec((POOL_HALO, d),
                       lambda i: (jnp.minimum((i + 1) * per_tile, n_halo_blocks - 1), 0))
    arrays, specs = zip(*[_operand(p) for p in pool_params])
    return _ffn_call(functools.partial(_pool_ffn_body, seq_len), "pool_ffn", x, ffn_params,
                     pre_args=(x, x, *arrays), pre_specs=(prev, nxt, *specs),
                     extra_scratch=(pltpu.VMEM((FFN_TM + 2 * POOL_HALO, d), F32),))


def _gelu(z):
    return 0.5 * z * (1.0 + lax.erf(z * (1.0 / math.sqrt(2.0))))


ROPE_HALF = HEAD_DIM // 2


def _qk_lane_map():
    lane = np.arange(LANES)
    return (lane // ROPE_HALF) % 2, (lane // HEAD_DIM) * ROPE_HALF + lane % ROPE_HALF


def _qk_head0_lanes():
    lane = lax.broadcasted_iota(jnp.int32, (1, LANES), 1)
    return (lane // ROPE_HALF) % 2 == 0


def _head_rms(t, gain, head0):
    sq = t * t
    ms_a = jnp.sum(jnp.where(head0, sq, 0.0), axis=-1, keepdims=True) * (1.0 / HEAD_DIM)
    ms_b = jnp.sum(jnp.where(head0, 0.0, sq), axis=-1, keepdims=True) * (1.0 / HEAD_DIM)
    r = jnp.where(head0, lax.rsqrt(ms_a + RMS_EPS), lax.rsqrt(ms_b + RMS_EPS))
    return t * r * gain


def _rope(t, cos, sin_signed):
    return t * cos + pltpu.roll(t, LANES // 2, axis=1) * sin_signed


def _proj_body(x_ref, gain_ref, win_ref, sgun_ref, ws_ref, bs_ref, qn_ref, kn_ref,
               cos_ref, sin_ref, a_ref, q_ref, k_ref, v_ref):
    h = _rms_rows(x_ref[...], gain_ref[...]).astype(BF16)

    def proj(lo, width):
        return jnp.dot(h, win_ref[:, lo:lo + width], preferred_element_type=F32)

    zq = proj(2 * A_WIDTH, B_WIDTH)
    zk = proj(2 * A_WIDTH + B_WIDTH, B_WIDTH)

    gv = _gelu(proj(A_WIDTH, A_WIDTH))
    u = _gelu(proj(0, A_WIDTH))
    n_chunks = PROJ_TM // SGU_CHUNK
    for g in range(A_GROUPS):
        cols = slice(g * A_GROUP_DIM, (g + 1) * A_GROUP_DIM)
        gvn = _rms_rows(gv[:, cols], sgun_ref[:, cols]).astype(BF16)
        rhs = jnp.concatenate(
            [gvn[c * SGU_CHUNK:(c + 1) * SGU_CHUNK, :] for c in range(n_chunks)], axis=1)
        mixed = jnp.dot(ws_ref[g], rhs, preferred_element_type=F32)
        for c in range(n_chunks):
            rows = slice(c * SGU_CHUNK, (c + 1) * SGU_CHUNK)
            m = mixed[:, c * A_GROUP_DIM:(c + 1) * A_GROUP_DIM] + bs_ref[g]
            a_ref[rows, cols] = (u[rows, cols] * m).astype(a_ref.dtype)

    cos, sin_signed = cos_ref[...], sin_ref[...]
    head0 = _qk_head0_lanes()
    scale = 1.0 / math.sqrt(HEAD_DIM)
    for j in range(B_WIDTH // LANES):
        cols = slice(j * LANES, (j + 1) * LANES)
        qn = _head_rms(zq[:, cols], qn_ref[:, cols], head0)
        kn = _head_rms(zk[:, cols], kn_ref[:, cols], head0)
        q_ref[:, cols] = (_rope(qn, cos, sin_signed) * scale).astype(q_ref.dtype)
        k_ref[:, cols] = _rope(kn, cos, sin_signed).astype(k_ref.dtype)
    v_ref[...] = proj(2 * A_WIDTH + 2 * B_WIDTH, B_WIDTH).astype(v_ref.dtype)


def _rope_tables(seq_len):
    pos = jnp.arange(seq_len, dtype=F32)
    inv_freq = ROPE_THETA ** (-jnp.arange(0, HEAD_DIM, 2, dtype=F32) / HEAD_DIM)
    ang = jnp.tile(pos[:, None] * inv_freq[None, :], (1, LANES // ROPE_HALF))
    first_half = jnp.asarray(_qk_lane_map()[1] < ROPE_HALF)
    sin = jnp.sin(ang)
    return jnp.cos(ang), jnp.where(first_half, -sin, sin)


def _to_qk_order(a):
    lead = a.shape[:-1]
    a = a.reshape(*lead, B_HEADS // 2, 2, 2, ROPE_HALF)
    return jnp.swapaxes(a, -3, -2).reshape(*lead, B_WIDTH)


def _even_proj(x, seq_len, proj_params):
    t, d = x.shape
    tiles_per_seq = seq_len // PROJ_TM
    cos, sin_signed = _rope_tables(seq_len)
    row = lambda w: pl.BlockSpec((PROJ_TM, w), lambda i: (i, 0))
    tab = pl.BlockSpec((PROJ_TM, LANES), lambda i: (i % tiles_per_seq, 0))
    arrays, specs = zip(*[_operand(p) for p in proj_params])
    return pl.pallas_call(
        _proj_body,
        out_shape=(jax.ShapeDtypeStruct((t, A_WIDTH), BF16),
                   *[jax.ShapeDtypeStruct((t, B_WIDTH), F32)] * 3),
        grid=(t // PROJ_TM,),
        in_specs=[row(d), *specs, tab, tab],
        out_specs=(row(A_WIDTH), row(B_WIDTH), row(B_WIDTH), row(B_WIDTH)),
        compiler_params=_params(1),
        name="even_proj",
    )(x, *arrays, cos, sin_signed)


def _attn_body(seq_len, q_ref, k_ref, v_ref, bias_ref, o_ref, *scratch):
    n_branches = len(DILATIONS)
    kv_res = [scratch[2 * i:2 * i + 2] for i in range(n_branches)]
    kv_stage = scratch[2 * n_branches:2 * n_branches + 2]
    num_ref, max_ref, den_ref = scratch[2 * n_branches + 2:]

    d1 = DILATIONS[1]
    assert DILATIONS == (1, d1, d1 * d1)
    seg1, seg2 = seq_len // d1, seq_len // (d1 * d1)
    zeros = jnp.zeros((HALF_STEPS, LANES), BF16)

    def put_segment(dst, index, rows_bf16):
        seg = rows_bf16.shape[0]
        base = index * (seg + 2 * HALF_STEPS)
        dst[base:base + HALF_STEPS, :] = zeros
        dst[base + HALF_STEPS:base + HALF_STEPS + seg, :] = rows_bf16
        dst[base + HALF_STEPS + seg:base + seg + 2 * HALF_STEPS, :] = zeros

    for which, (src, stage) in enumerate(zip((k_ref, v_ref), kv_stage)):
        put_segment(kv_res[0][which], 0, src[...].astype(BF16))
        for r in range(d1):
            stage[r * seg1:(r + 1) * seg1, :] = src[pl.ds(r, seg1, stride=d1), :]
        for r in range(d1):
            put_segment(kv_res[1][which], r, stage[r * seg1:(r + 1) * seg1, :].astype(BF16))
            for hop in range(d1):
                rows = stage[pl.ds(r * seg1 + hop, seg2, stride=d1), :]
                put_segment(kv_res[2][which], r + d1 * hop, rows.astype(BF16))

    head0_qk = _qk_head0_lanes()
    head0_v = lax.broadcasted_iota(jnp.int32, (1, LANES), 1) < HEAD_DIM

    def one_block(branch, dil, n_blocks, flat):
        r = flat // n_blocks
        n = flat % n_blocks
        q_start = n * (ATT_QB * dil) + r
        if dil == 1:
            q_rows = pl.ds(pl.multiple_of(q_start, ATT_QB), ATT_QB)
        else:
            q_rows = pl.ds(q_start, ATT_QB, stride=dil)
        pitch = seq_len // dil + 2 * HALF_STEPS
        k_rows = pl.ds(pl.multiple_of(r * pitch + n * ATT_QB, ATT_QB), ATT_KW)
        kres, vres = kv_res[branch]
        q = q_ref[q_rows, :]
        kw = kres[k_rows, :]
        vw = vres[k_rows, :]
        q2 = jnp.concatenate([jnp.where(head0_qk, q, 0.0), jnp.where(head0_qk, 0.0, q)],
                             axis=0).astype(BF16)
        s = lax.dot_general(q2, kw, (((1,), (1,)), ((), ())), preferred_element_type=F32)
        edge = jnp.where(n == 0, 1, 0) + jnp.where(n == n_blocks - 1, 2, 0)
        s = s + bias_ref[edge]
        m = jnp.max(s, axis=-1, keepdims=True)
        p = jnp.exp(s - m)
        l = jnp.sum(p, axis=-1, keepdims=True)
        o2 = jnp.dot(p.astype(BF16), vw, preferred_element_type=F32)
        nums = [jnp.where(head0_v, o2[:ATT_QB], o2[ATT_QB:])]
        maxima = [jnp.where(head0_v, m[:ATT_QB], m[ATT_QB:])]
        dens = [jnp.where(head0_v, l[:ATT_QB], l[ATT_QB:])]
        if branch > 0:
            num_ref[branch - 1, q_rows, :] = nums[0]
            max_ref[branch - 1, q_rows, :] = maxima[0]
            den_ref[branch - 1, q_rows, :] = dens[0]
            return
        for b in range(n_branches - 1):
            nums.append(num_ref[b, q_rows, :])
            maxima.append(max_ref[b, q_rows, :])
            dens.append(den_ref[b, q_rows, :])
        m_all = functools.reduce(jnp.maximum, maxima)
        weights = [jnp.exp(m_b - m_all) for m_b in maxima]
        num = sum(w * n_b for w, n_b in zip(weights, nums))
        den = sum(w * d_b for w, d_b in zip(weights, dens))
        o_ref[q_rows, :] = (num / den).astype(o_ref.dtype)

    assert DILATIONS[0] == 1
    for branch in reversed(range(n_branches)):
        dil = DILATIONS[branch]
        n_blocks = seq_len // dil // ATT_QB

        def group(t, carry, branch=branch, dil=dil, n_blocks=n_blocks):
            for g in range(ATT_GROUP):
                one_block(branch, dil, n_blocks, t * ATT_GROUP + g)
            return carry

        lax.fori_loop(0, seq_len // ATT_QB // ATT_GROUP, group, 0)


def _attn_bias():
    qi = np.arange(2 * ATT_QB)[:, None] % ATT_QB
    kj = np.arange(ATT_KW)[None, :]
    band = np.abs(kj - HALF_STEPS - qi) <= HALF_STEPS
    first = kj >= HALF_STEPS
    last = kj < ATT_KW - HALF_STEPS
    variants = (band, band & first, band & last, band & first & last)
    return jnp.asarray(np.stack([np.where(v, 0.0, NEG_BIG) for v in variants]), F32)


def _attention(q, k, v, batch, seq_len):
    t, w = q.shape
    bias = _attn_bias()
    shape3 = (batch, seq_len, w)
    slab = pl.BlockSpec((None, seq_len, LANES), lambda b, j: (b, 0, j))
    kv_scratch = [pltpu.VMEM((seq_len + 2 * HALF_STEPS * dil, LANES), BF16)
                  for dil in DILATIONS for _ in ("k", "v")]
    stage = pltpu.VMEM((seq_len, LANES), F32)
    state = pltpu.VMEM((len(DILATIONS) - 1, seq_len, LANES), F32)
    out = pl.pallas_call(
        functools.partial(_attn_body, seq_len),
        out_shape=jax.ShapeDtypeStruct(shape3, BF16),
        grid=(batch, w // LANES),
        in_specs=[slab, slab, slab, _resident(bias.shape)],
        out_specs=slab,
        scratch_shapes=[*kv_scratch, stage, stage, state, state, state],
        compiler_params=_params(2),
        name="dilated_attn",
    )(q.reshape(shape3), k.reshape(shape3), v.reshape(shape3), bias)
    return out.reshape(t, w)


def kernel(x, ffn1_norm, ffn1_w_gate, ffn1_w_up, ffn1_w_down, mix_norm, even_w_in, sgu_norm,
           sgu_w_spatial, sgu_b_spatial, attn_q_norm, attn_k_norm, even_w_out, pool_w_group,
           pool_scale, ffn2_norm, ffn2_w_gate, ffn2_w_up, ffn2_w_down):
    batch, seq_len, d = x.shape
    depth = ffn1_norm.shape[0]
    assert seq_len % (ATT_QB * max(DILATIONS)) == 0 and seq_len % (ATT_QB * ATT_GROUP) == 0
    assert seq_len % FFN_TM == 0 and seq_len % PROJ_TM == 0 and d == D_MODEL
    cast = lambda w: w.astype(BF16)
    ffn1 = (_row_vectors(ffn1_norm), cast(ffn1_w_gate), cast(ffn1_w_up), cast(0.5 * ffn1_w_down))
    ffn2 = (_row_vectors(ffn2_norm), cast(ffn2_w_gate), cast(ffn2_w_up), cast(0.5 * ffn2_w_down))
    mix_gain = _row_vectors(mix_norm)
    n_even = even_w_in.shape[0]
    q_lo, k_lo, v_lo = 2 * A_WIDTH, 2 * A_WIDTH + B_WIDTH, 2 * A_WIDTH + 2 * B_WIDTH
    w_in = jnp.concatenate([even_w_in[..., :q_lo], _to_qk_order(even_w_in[..., q_lo:k_lo]),
                            _to_qk_order(even_w_in[..., k_lo:v_lo]), even_w_in[..., v_lo:]],
                           axis=-1)
    proj = (cast(w_in), sgu_norm.reshape(n_even, 1, A_WIDTH), cast(sgu_w_spatial),
            jnp.broadcast_to(sgu_b_spatial[..., None], sgu_b_spatial.shape + (A_GROUP_DIM,)),
            _row_vectors(_to_qk_order(jnp.tile(attn_q_norm, (1, B_HEADS)))),
            _row_vectors(_to_qk_order(jnp.tile(attn_k_norm, (1, B_HEADS)))))
    w_out = cast(even_w_out)
    pool = (cast(pool_w_group), _row_vectors(pool_scale))

    pick = lambda stacked, index: [_Layer(p, index) for p in stacked]
    xf = x.reshape(batch * seq_len, d)
    for layer in range(depth):
        xf = _ffn(xf, pick(ffn1, layer))
        j = layer // 2
        if layer % 2 == 0:
            a, q, k, v = _even_proj(xf, seq_len, [_Layer(mix_gain, layer), *pick(proj, j)])
            b = _attention(q, k, v, batch, seq_len)
            xf = _mix_ffn(xf, a, b, _Layer(w_out, j), pick(ffn2, layer))
        else:
            xf = _pool_ffn(xf, seq_len, [_Layer(mix_gain, layer), *pick(pool, j)],
                           pick(ffn2, layer))
    return xf.reshape(batch, seq_len, d)
```
